```python
import jax, jax.numpy as jnp
from jax import lax
import numpy as np

D_MODEL = 2048
BATCH = 16
SEQ = 2048
DEPTH = 1

GRID_W = 64
CTX_LEN = 256
D_RET = 1024
RET_HEADS = 4
RET_HD = D_RET // RET_HEADS
RET_CHUNK = 128
D_LRU = 1024
LRU_BLOCKS = 8
LRU_BD = D_LRU // LRU_BLOCKS
CONV_W = 4
CONV_PAD_L = 1
CONV_PAD_R = 2
LRU_C = 8.0
D_MIX = D_RET + D_LRU
D_IN = 4 * D_RET + 2 * D_LRU
N_EXPERTS = 32
TOP_K = 4
D_FF = D_MODEL
SWIGLU_LIMIT = 7.0
SWIGLU_ALPHA = 1.702
MOE_BLOCK = 128
ROPE_THETA = 10000.0
EPS = 1e-6

kernel_name = 'hybrid_retention_rglru_moe_dit_layer'


def _rmsnorm(x, g):
    xf = x.astype(jnp.float32)
    y = xf * lax.rsqrt(jnp.mean(xf * xf, axis=-1, keepdims=True) + EPS)
    return y.astype(x.dtype) * g


def _ada(cond, w, b):
    return jax.nn.silu(cond) @ w + b


def _rope_1d(x, pos):
    half = x.shape[-1] // 2
    inv = ROPE_THETA ** (-jnp.arange(half, dtype=jnp.float32) / half)
    ang = pos.astype(jnp.float32)[:, None] * inv[None, :]
    cos = jnp.cos(ang).astype(x.dtype)
    sin = jnp.sin(ang).astype(x.dtype)
    x1, x2 = x[..., :half], x[..., half:]
    return jnp.concatenate([x1 * cos - x2 * sin, x1 * sin + x2 * cos], axis=-1)


def _rope2d(x, rows, cols):
    hd = x.shape[-1]
    return jnp.concatenate([_rope_1d(x[..., :hd // 2], rows), _rope_1d(x[..., hd // 2:], cols)], axis=-1)


def _ret_log_gamma():
    return jnp.log1p(-(2.0 ** (-5.0 - jnp.arange(RET_HEADS, dtype=jnp.float32))))


def _retention_dir(q, k, v, s0, include_diag):
    B, H, L, hd = q.shape
    C = RET_CHUNK
    N = L // C
    log_g = _ret_log_gamma()
    idx = jnp.arange(C, dtype=jnp.float32)
    diff = idx[:, None] - idx[None, :]
    mask = (diff >= 0) if include_diag else (diff > 0)
    intra_decay = jnp.where(mask[None], jnp.exp(jnp.where(mask, diff, 0.0)[None] * log_g[:, None, None]), 0.0)
    q_decay = jnp.exp((idx + 1.0)[None, :] * log_g[:, None])
    k_decay = jnp.exp((C - 1.0 - idx)[None, :] * log_g[:, None])
    chunk_decay = jnp.exp(C * log_g)

    def to_chunks(t):
        return t.reshape(B, H, N, C, hd).transpose(2, 0, 1, 3, 4)

    def step(s, qkv):
        qc, kc, vc = qkv
        qf, kf, vf = qc.astype(jnp.float32), kc.astype(jnp.float32), vc.astype(jnp.float32)
        scores = jnp.einsum('bhid,bhjd->bhij', qf, kf) * intra_decay[None]
        o = jnp.einsum('bhij,bhjd->bhid', scores, vf) + jnp.einsum('bhid,bhde->bhie', qf, s) * q_decay[None, :, :, None]
        s_new = s * chunk_decay[None, :, None, None] + jnp.einsum('bhjd,bhje->bhde', kf * k_decay[None, :, :, None], vf)
        return s_new, o.astype(q.dtype)

    s_fin, o = lax.scan(step, s0, (to_chunks(q), to_chunks(k), to_chunks(v)))
    o = o.transpose(1, 2, 0, 3, 4).reshape(B, H, L, hd)
    return o, s_fin


def _retention_bidir(q, k, v, s0_f, s0_b):
    o_f, s_f = _retention_dir(q, k, v, s0_f, True)
    flip = lambda t: jnp.flip(t, axis=2)
    o_b, s_b = _retention_dir(flip(q), flip(k), flip(v), s0_b, False)
    return o_f + flip(o_b), s_f, s_b


def _conv_centered(u, w, b):
    L = u.shape[1]
    up = jnp.pad(u, ((0, 0), (CONV_PAD_L, CONV_PAD_R), (0, 0)))
    out = b
    for tap in range(CONV_W):
        out = out + up[:, tap:tap + L] * w[tap]
    return out


def _rglru_dir(u, h0, w_a, b_a, w_x, b_x, lam):
    B, L, Dl = u.shape
    ub = u.reshape(B, L, LRU_BLOCKS, LRU_BD)
    r = jax.nn.sigmoid(jnp.einsum('blhi,hij->blhj', ub, w_a).reshape(B, L, Dl) + b_a)
    i = jax.nn.sigmoid(jnp.einsum('blhi,hij->blhj', ub, w_x).reshape(B, L, Dl) + b_x)
    log_a = (-LRU_C * jax.nn.softplus(-lam.astype(jnp.float32))) * r.astype(jnp.float32)
    a = jnp.exp(log_a)
    bterm = jnp.sqrt(-jnp.expm1(2.0 * log_a)) * (i * u).astype(jnp.float32)
    bterm = bterm.at[:, 0].add(a[:, 0] * h0)

    def comb(e1, e2):
        a1, b1 = e1
        a2, b2 = e2
        return a1 * a2, a2 * b1 + b2

    _, h = lax.associative_scan(comb, (a, bterm), axis=1)
    return h, h[:, -1]


def _rglru_bidir(u, h0_f, h0_b, p):
    h_f, last_f = _rglru_dir(u, h0_f, p['lru_w_a'][0], p['lru_b_a'][0], p['lru_w_x'][0], p['lru_b_x'][0], p['lru_lambda'][0])
    h_b, last_b = _rglru_dir(jnp.flip(u, axis=1), h0_b, p['lru_w_a'][1], p['lru_b_a'][1], p['lru_w_x'][1], p['lru_b_x'][1], p['lru_lambda'][1])
    return h_f + jnp.flip(h_b, axis=1), last_f, last_b


def _head_groupnorm(o):
    of = o.astype(jnp.float32)
    mu = jnp.mean(of, axis=-1, keepdims=True)
    var = jnp.mean(jnp.square(of - mu), axis=-1, keepdims=True)
    return ((of - mu) * lax.rsqrt(var + EPS)).astype(o.dtype)


def _merge(o_ret, g, h_lru, y, p):
    B, H, L, hd = o_ret.shape
    ret = _head_groupnorm(o_ret).transpose(0, 2, 1, 3).reshape(B, L, H * hd)
    ret = ret * p['ret_norm_g'] * jax.nn.silu(g)
    lru = _rmsnorm(h_lru.astype(y.dtype) * jax.nn.gelu(y), p['lru_norm_g'])
    return jnp.concatenate([ret, lru], axis=-1) @ p['w_out']


def _mixer(hx, hc, p, need_ctx_out):
    B, L, _ = hx.shape
    pos = jnp.arange(L)
    rows = pos // GRID_W
    cols = pos % GRID_W
    splits = [D_RET, 2 * D_RET, 3 * D_RET, 4 * D_RET, 4 * D_RET + D_LRU]
    qx, kx, vx, gx, ux, yx = jnp.split(hx @ p['w_in'], splits, axis=-1)
    qc, kc, vc, gc, uc, yc = jnp.split(hc @ p['w_in'], splits, axis=-1)

    def heads(t):
        return t.reshape(t.shape[0], t.shape[1], RET_HEADS, RET_HD).transpose(0, 2, 1, 3)

    scale = RET_HD ** -0.5
    s_zero = jnp.zeros((B, RET_HEADS, RET_HD, RET_HD), jnp.float32)
    oc, s_f, s_b = _retention_bidir(heads(qc), heads(kc) * scale, heads(vc), s_zero, s_zero)
    qx_h = _rope2d(heads(qx), rows, cols)
    kx_h = _rope2d(heads(kx), rows, cols) * scale
    ox, _, _ = _retention_bidir(qx_h, kx_h, heads(vx), s_f, s_b)
    h_zero = jnp.zeros((B, D_LRU), jnp.float32)
    hc_lru, h_f, h_b = _rglru_bidir(_conv_centered(uc, p['conv_w'], p['conv_b']), h_zero, h_zero, p)
    hx_lru, _, _ = _rglru_bidir(_conv_centered(ux, p['conv_w'], p['conv_b']), h_f, h_b, p)
    out_x = _merge(ox, gx, hx_lru, yx, p)
    out_c = _merge(oc, gc, hc_lru, yc, p) if need_ctx_out else None
    return out_x, out_c


def _moe(h, p):
    B, L, D = h.shape
    T = B * L
    TK = T * TOP_K
    hf = h.reshape(T, D)
    logits = (hf @ p['router_w'] + p['router_b']).astype(jnp.float32)
    top_v, top_i = lax.top_k(logits, TOP_K)
    gates = jax.nn.softmax(top_v, axis=-1).astype(h.dtype)
    flat_e = top_i.reshape(-1)
    flat_g = gates.reshape(-1)
    order = jnp.argsort(flat_e)
    sorted_e = flat_e[order]
    counts = jnp.bincount(flat_e, length=N_EXPERTS)
    padded = (counts + MOE_BLOCK - 1) // MOE_BLOCK * MOE_BLOCK
    starts = jnp.cumsum(counts) - counts
    pad_end = jnp.cumsum(padded)
    pad_starts = pad_end - padded
    dest = pad_starts[sorted_e] + (jnp.arange(TK) - starts[sorted_e])
    n_blocks = -(-TK // MOE_BLOCK) + N_EXPERTS
    P = n_blocks * MOE_BLOCK
    buf_tok = jnp.full((P,), T, jnp.int32).at[dest].set((order // TOP_K).astype(jnp.int32))
    buf_gate = jnp.zeros((P,), h.dtype).at[dest].set(flat_g[order])
    blk_e = jnp.clip(jnp.searchsorted(pad_end, jnp.arange(n_blocks) * MOE_BLOCK, side='right'), 0, N_EXPERTS - 1)
    h_pad = jnp.concatenate([hf, jnp.zeros((1, D), h.dtype)], axis=0)

    def body(acc, xs):
        tok, gate, e = xs
        xt = h_pad[tok]
        gl = xt @ p['w_gate'][e] + p['b_gate'][e]
        up = xt @ p['w_up'][e] + p['b_up'][e]
        gl = jnp.minimum(gl, SWIGLU_LIMIT)
        up = jnp.clip(up, -SWIGLU_LIMIT, SWIGLU_LIMIT)
        act = (up + 1.0) * gl * jax.nn.sigmoid(SWIGLU_ALPHA * gl)
        y = act @ p['w_down'][e] + p['b_down'][e]
        return acc.at[tok].add(y * gate[:, None]), None

    acc, _ = lax.scan(body, jnp.zeros((T + 1, D), h.dtype),
                      (buf_tok.reshape(n_blocks, MOE_BLOCK), buf_gate.reshape(n_blocks, MOE_BLOCK), blk_e))
    return acc[:T].reshape(B, L, D)


def setup_inputs(seed: int = 0) -> dict:
    key = jax.random.key(seed)
    ks = jax.random.split(key, 32)
    f32 = jnp.float32
    nrm = lambda k, shape, s: jax.random.normal(k, shape, f32) * s
    a0 = jax.random.uniform(ks[14], (DEPTH, 2, D_LRU), f32, minval=0.9, maxval=0.999)
    pa = a0 ** (1.0 / LRU_C)
    lam = jnp.log(pa) - jnp.log1p(-pa)
    return {
        'x': nrm(ks[0], (BATCH, SEQ, D_MODEL), 1.0),
        'c': nrm(ks[1], (BATCH, D_MODEL), 1.0),
        'ctx': nrm(ks[2], (BATCH, CTX_LEN, D_MODEL), 1.0),
        'c_ctx': nrm(ks[3], (D_MODEL,), 1.0),
        'ada_w': nrm(ks[4], (DEPTH, D_MODEL, 6 * D_MODEL), 0.5 * D_MODEL ** -0.5),
        'ada_b': nrm(ks[5], (DEPTH, 6 * D_MODEL), 0.01),
        'norm1_g': 1.0 + nrm(ks[6], (DEPTH, D_MODEL), 0.02),
        'norm2_g': 1.0 + nrm(ks[7], (DEPTH, D_MODEL), 0.02),
        'w_in': nrm(ks[8], (DEPTH, D_MODEL, D_IN), D_MODEL ** -0.5),
        'conv_w': nrm(ks[9], (DEPTH, CONV_W, D_LRU), CONV_W ** -0.5),
        'conv_b': nrm(ks[10], (DEPTH, D_LRU), 0.01),
        'lru_w_a': nrm(ks[11], (DEPTH, 2, LRU_BLOCKS, LRU_BD, LRU_BD), LRU_BD ** -0.5),
        'lru_b_a': nrm(ks[12], (DEPTH, 2, D_LRU), 0.01),
        'lru_w_x': nrm(ks[13], (DEPTH, 2, LRU_BLOCKS, LRU_BD, LRU_BD), LRU_BD ** -0.5),
        'lru_b_x': nrm(ks[15], (DEPTH, 2, D_LRU), 0.01),
        'lru_lambda': lam,
        'ret_norm_g': 1.0 + nrm(ks[16], (DEPTH, D_RET), 0.02),
        'lru_norm_g': 1.0 + nrm(ks[17], (DEPTH, D_LRU), 0.02),
        'w_out': nrm(ks[18], (DEPTH, D_MIX, D_MODEL), D_MIX ** -0.5),
        'router_w': nrm(ks[19], (DEPTH, D_MODEL, N_EXPERTS), D_MODEL ** -0.5),
        'router_b': nrm(ks[20], (DEPTH, N_EXPERTS), 0.01),
        'w_gate': nrm(ks[21], (DEPTH, N_EXPERTS, D_MODEL, D_FF), D_MODEL ** -0.5),
        'b_gate': nrm(ks[22], (DEPTH, N_EXPERTS, D_FF), 0.01),
        'w_up': nrm(ks[23], (DEPTH, N_EXPERTS, D_MODEL, D_FF), D_MODEL ** -0.5),
        'b_up': nrm(ks[24], (DEPTH, N_EXPERTS, D_FF), 0.01),
        'w_down': nrm(ks[25], (DEPTH, N_EXPERTS, D_FF, D_MODEL), D_FF ** -0.5),
        'b_down': nrm(ks[26], (DEPTH, N_EXPERTS, D_MODEL), 0.01),
        'final_norm_g': 1.0 + nrm(ks[27], (D_MODEL,), 0.02),
    }


def reference(x, c, ctx, c_ctx, ada_w, ada_b, norm1_g, norm2_g, w_in, conv_w, conv_b,
              lru_w_a, lru_b_a, lru_w_x, lru_b_x, lru_lambda, ret_norm_g, lru_norm_g, w_out,
              router_w, router_b, w_gate, b_gate, w_up, b_up, w_down, b_down, final_norm_g):
    for l in range(DEPTH):
        last = l == DEPTH - 1
        mix_p = {'w_in': w_in[l], 'conv_w': conv_w[l], 'conv_b': conv_b[l],
                 'lru_w_a': lru_w_a[l], 'lru_b_a': lru_b_a[l], 'lru_w_x': lru_w_x[l], 'lru_b_x': lru_b_x[l],
                 'lru_lambda': lru_lambda[l], 'ret_norm_g': ret_norm_g[l], 'lru_norm_g': lru_norm_g[l],
                 'w_out': w_out[l]}
        moe_p = {'router_w': router_w[l], 'router_b': router_b[l], 'w_gate': w_gate[l], 'b_gate': b_gate[l],
                 'w_up': w_up[l], 'b_up': b_up[l], 'w_down': w_down[l], 'b_down': b_down[l]}
        mx = _ada(c, ada_w[l], ada_b[l])[:, None, :]
        mc = _ada(c_ctx, ada_w[l], ada_b[l])
        sh1, sc1, g1, sh2, sc2, g2 = jnp.split(mx, 6, axis=-1)
        csh1, csc1, cg1, csh2, csc2, cg2 = jnp.split(mc, 6, axis=-1)
        hx = _rmsnorm(x, norm1_g[l]) * (1.0 + sc1) + sh1
        hc = _rmsnorm(ctx, norm1_g[l]) * (1.0 + csc1) + csh1
        mix_x, mix_c = _mixer(hx, hc, mix_p, not last)
        x = x + g1 * mix_x
        x = x + g2 * _moe(_rmsnorm(x, norm2_g[l]) * (1.0 + sc2) + sh2, moe_p)
        if not last:
            ctx = ctx + cg1 * mix_c
            ctx = ctx + cg2 * _moe(_rmsnorm(ctx, norm2_g[l]) * (1.0 + csc2) + csh2, moe_p)
    return _rmsnorm(x, final_norm_g)
```

```python
import functools

import jax
import jax.numpy as jnp
import numpy as np
from jax import lax
from jax.experimental import pallas as pl
from jax.experimental.pallas import tpu as pltpu

RET_HEADS = 4
LRU_BLOCKS = 8
LRU_BD = 128
CONV_W = 4
GRID_W = 64
TOP_K = 4
LRU_C = 8.0
SWIGLU_LIMIT = 7.0
SWIGLU_ALPHA = 1.702
ROPE_THETA = 10000.0
EPS = 1e-6

LANES = 128
SUBLANES = 8
VMEM_LIMIT = 56 * 1024 * 1024

F32 = jnp.float32
BF16 = jnp.bfloat16


def _cparams(sem):
    return pltpu.CompilerParams(dimension_semantics=sem, vmem_limit_bytes=VMEM_LIMIT)


def _sigmoid(z):
    return 1.0 / (1.0 + jnp.exp(-z))


def _ada_kernel(c_ref, w_ref, b_ref, o_ref):
    c = c_ref[...]
    s = (c * _sigmoid(c)).astype(BF16)
    o_ref[...] = jnp.dot(s, w_ref[...].astype(BF16), preferred_element_type=F32) + b_ref[...]


def _ada(cond, w, b, tn):
    R, D = cond.shape
    N = w.shape[1]
    return pl.pallas_call(
        _ada_kernel,
        grid=(N // tn,),
        in_specs=[
            pl.BlockSpec((R, D), lambda j: (0, 0)),
            pl.BlockSpec((D, tn), lambda j: (0, j)),
            pl.BlockSpec((1, tn), lambda j: (0, j)),
        ],
        out_specs=pl.BlockSpec((R, tn), lambda j: (0, j)),
        out_shape=jax.ShapeDtypeStruct((R, N), F32),
        compiler_params=_cparams(("arbitrary",)),
        name="ada",
    )(cond, w, b.reshape(1, N))


def _in_proj_kernel(*refs, kinds, hd, scale):
    n_out = len(kinds)
    x_ref, sc_ref, sh_ref, g_ref, w_ref, cos_ref, sin_ref = refs[:7]
    outs = refs[7:7 + n_out]
    xn_ref = refs[7 + n_out]
    n = pl.program_id(1)

    @pl.when(n == 0)
    def _():
        x = x_ref[...]
        y = x * lax.rsqrt(jnp.mean(x * x, axis=-1, keepdims=True) + EPS)
        y = y * g_ref[...]
        y = y * (1.0 + sc_ref[0]) + sh_ref[0]
        xn_ref[...] = y.astype(BF16)

    acc = jnp.dot(xn_ref[...], w_ref[...], preferred_element_type=F32)

    def rope(a):
        pieces = []
        n_grp = a.shape[1] // LANES
        per_head = hd // LANES
        for gi in range(n_grp):
            sl = a[:, gi * LANES:(gi + 1) * LANES]
            t = gi % per_head
            c = cos_ref[:, t * LANES:(t + 1) * LANES]
            s = sin_ref[:, t * LANES:(t + 1) * LANES]
            pieces.append(sl * c + pltpu.roll(sl, LANES // 2, 1) * s)
        return jnp.concatenate(pieces, axis=1)

    for idx, kind in enumerate(kinds):
        @pl.when(n == idx)
        def _(idx=idx, kind=kind):
            o = outs[idx]
            if kind == "q_rope":
                o[...] = rope(acc).astype(o.dtype)
            elif kind == "k_rope":
                o[...] = (rope(acc) * scale).astype(o.dtype)
            elif kind == "k_plain":
                o[...] = (acc * scale).astype(o.dtype)
            else:
                o[...] = acc.astype(o.dtype)


def _in_proj(x2d, mods3, mod_row_fn, norm_g, w_bf, cos_t, sin_t, kinds, dtypes, tm, seq_len, hd):
    T, D = x2d.shape
    n_out = len(kinds)
    cw = w_bf.shape[1] // n_out
    tiles_per_seq = seq_len // tm
    in_specs = [
        pl.BlockSpec((tm, D), lambda i, n: (i, 0)),
        pl.BlockSpec((1, 1, D), lambda i, n: (mod_row_fn(i), 0, 1)),
        pl.BlockSpec((1, 1, D), lambda i, n: (mod_row_fn(i), 0, 0)),
        pl.BlockSpec((1, D), lambda i, n: (0, 0)),
        pl.BlockSpec((D, cw), lambda i, n: (0, n)),
        pl.BlockSpec((tm, hd), lambda i, n: (i % tiles_per_seq, 0)),
        pl.BlockSpec((tm, hd), lambda i, n: (i % tiles_per_seq, 0)),
    ]
    out_specs = [pl.BlockSpec((tm, cw), lambda i, n: (i, 0)) for _ in kinds]
    out_shape = [jax.ShapeDtypeStruct((T, cw), dt) for dt in dtypes]
    return pl.pallas_call(
        functools.partial(_in_proj_kernel, kinds=tuple(kinds), hd=hd, scale=float(hd) ** -0.5),
        grid=(T // tm, n_out),
        in_specs=in_specs,
        out_specs=out_specs,
        out_shape=out_shape,
        scratch_shapes=[pltpu.VMEM((tm, D), BF16)],
        compiler_params=_cparams(("arbitrary", "arbitrary")),
        name="in_proj",
    )(x2d, mods3, mods3, norm_g.reshape(1, D), w_bf, cos_t, sin_t)


def _dot_t0(a, b):
    return lax.dot_general(a, b, (((0,), (0,)), ((), ())), preferred_element_type=F32)


def _dot_t1(a, b):
    return lax.dot_general(a, b, (((1,), (1,)), ((), ())), preferred_element_type=F32)


def _ret_kernel(lg_ref, q_ref, k_ref, v_ref, g_ref, kc_ref, vc_ref, gn_ref, o_ref,
                oacc_ref, s_ref, sf_ref, *, C, n_chunks, n_cchunks):
    hd = q_ref.shape[1]
    lg = lg_ref[pl.program_id(1)]
    ri = lax.broadcasted_iota(jnp.int32, (C, C), 0).astype(F32)
    ci = lax.broadcasted_iota(jnp.int32, (C, C), 1).astype(F32)
    dsym = jnp.exp(jnp.abs(ri - ci) * lg)
    pos = lax.broadcasted_iota(jnp.int32, (C, hd), 0).astype(F32)
    qd_f = jnp.exp((pos + 1.0) * lg)
    qd_b = jnp.exp((float(C) - pos) * lg)
    kd_f = jnp.exp((float(C) - 1.0 - pos) * lg)
    kd_b = jnp.exp(pos * lg)
    cd = jnp.exp(jnp.full((1, 1), float(C), F32) * lg)

    def kv_update(s, kk, vv, kd):
        kw = (kk.astype(F32) * kd).astype(BF16)
        upd = _dot_t0(kw, vv)
        return upd if s is None else s * cd + upd

    s_f = None
    for c in range(n_cchunks):
        s_f = kv_update(s_f, kc_ref[c * C:(c + 1) * C, :], vc_ref[c * C:(c + 1) * C, :], kd_f)
    sf_ref[...] = s_f
    s_b = None
    for c in reversed(range(n_cchunks)):
        s_b = kv_update(s_b, kc_ref[c * C:(c + 1) * C, :], vc_ref[c * C:(c + 1) * C, :], kd_b)

    s_ref[...] = s_b

    def bwd(idx, carry):
        n = n_chunks - 1 - idx
        rows = pl.ds(pl.multiple_of(n * C, C), C)
        s = s_ref[...]
        oacc_ref[rows, :] = jnp.dot(q_ref[rows, :], s.astype(BF16), preferred_element_type=F32) * qd_b
        s_ref[...] = kv_update(s, k_ref[rows, :], v_ref[rows, :], kd_b)
        return carry

    lax.fori_loop(0, n_chunks, bwd, 0)

    s_ref[...] = sf_ref[...]

    def fwd(n, carry):
        rows = pl.ds(pl.multiple_of(n * C, C), C)
        s = s_ref[...]
        qn = q_ref[rows, :]
        kn = k_ref[rows, :]
        vn = v_ref[rows, :]
        oacc_ref[rows, :] += jnp.dot(qn, s.astype(BF16), preferred_element_type=F32) * qd_f
        scores = (_dot_t1(qn, kn) * dsym).astype(BF16)
        o = oacc_ref[rows, :] + jnp.dot(scores, vn, preferred_element_type=F32)
        s_ref[...] = kv_update(s, kn, vn, kd_f)
        mu = jnp.mean(o, axis=-1, keepdims=True)
        d = o - mu
        var = jnp.mean(d * d, axis=-1, keepdims=True)
        y = d * lax.rsqrt(var + EPS)
        gate = g_ref[rows, :].astype(F32)
        y = y * gn_ref[...] * (gate * _sigmoid(gate))
        o_ref[rows, :] = y.astype(o_ref.dtype)
        return carry

    lax.fori_loop(0, n_chunks, fwd, 0)


def _retention(log_g, q, k, v, g, kc, vc, gn, B, L, Lc, hd):
    T = q.shape[0]
    H = q.shape[1] // hd
    C = 256
    blk = lambda b, h, *_: (b, h)
    grid_spec = pltpu.PrefetchScalarGridSpec(
        num_scalar_prefetch=1,
        grid=(B, H),
        in_specs=[
            pl.BlockSpec((L, hd), blk),
            pl.BlockSpec((L, hd), blk),
            pl.BlockSpec((L, hd), blk),
            pl.BlockSpec((L, hd), blk),
            pl.BlockSpec((Lc, hd), blk),
            pl.BlockSpec((Lc, hd), blk),
            pl.BlockSpec((1, hd), lambda b, h, *_: (0, h)),
        ],
        out_specs=pl.BlockSpec((L, hd), blk),
        scratch_shapes=[pltpu.VMEM((L, hd), F32), pltpu.VMEM((hd, hd), F32), pltpu.VMEM((hd, hd), F32)],
    )
    return pl.pallas_call(
        functools.partial(_ret_kernel, C=C, n_chunks=L // C, n_cchunks=Lc // C),
        grid_spec=grid_spec,
        out_shape=jax.ShapeDtypeStruct((T, H * hd), BF16),
        compiler_params=_cparams(("arbitrary", "arbitrary")),
        name="retention",
    )(log_g, q, k, v, g, kc, vc, gn.reshape(1, H * hd))


def _gelu_tanh(y):
    return 0.5 * y * (1.0 + jnp.tanh(0.7978845608028654 * (y + 0.044715 * (y * y * y))))


def _lru_kernel(*refs, reverse, merge, B, tL, nT):
    if merge:
        (u_ref, up_ref, un_ref, h0_ref, cw_ref, cb_ref, w_ref, ba_ref, bx_ref, lam_ref,
         hf_ref, y_ref, gn_ref, out_ref, hlast_ref, ext_ref, a_ref, b_ref, ho_ref, hc_ref) = refs
    else:
        (u_ref, up_ref, un_ref, h0_ref, cw_ref, cb_ref, w_ref, ba_ref, bx_ref, lam_ref,
         out_ref, hlast_ref, ext_ref, a_ref, b_ref, ho_ref, hc_ref) = refs
    j = pl.program_id(0)
    tile = (nT - 1 - j) if reverse else j
    G = LRU_BLOCKS
    bd = LRU_BD
    HALO = SUBLANES

    @pl.when(j == 0)
    def _():
        for g in range(G):
            hc_ref[g] = h0_ref[:, g * bd:(g + 1) * bd]

    ext_ref[:, 0:HALO, :] = jnp.where(tile > 0, up_ref[...], 0.0)
    ext_ref[:, HALO:HALO + tL, :] = u_ref[...]
    ext_ref[:, HALO + tL:2 * HALO + tL, :] = jnp.where(tile < nT - 1, un_ref[...], 0.0)
    uc = jnp.zeros((B, tL, G * bd), F32) + cb_ref[...]
    for tap in range(CONV_W):
        off = HALO - 1 + tap
        uc = uc + ext_ref[:, off:off + tL, :] * cw_ref[tap:tap + 1, :]
    uc2 = uc.reshape(B * tL, G * bd)

    lam = lam_ref[...]
    nl = -lam
    softplus = jnp.maximum(nl, 0.0) + jnp.log(1.0 + jnp.exp(-jnp.abs(nl)))
    cfac = -LRU_C * softplus

    for g in range(G):
        sl = slice(g * bd, (g + 1) * bd)
        xg = uc2[:, sl]
        pre = jnp.dot(xg.astype(BF16), w_ref[g], preferred_element_type=F32)
        r = _sigmoid(pre[:, :bd] + ba_ref[:, sl])
        i = _sigmoid(pre[:, bd:] + bx_ref[:, sl])
        log_a = cfac[:, sl] * r
        a = jnp.exp(log_a)
        a_ref[g] = a
        b_ref[g] = jnp.sqrt(1.0 - jnp.exp(2.0 * log_a)) * (i * xg)

    def step(s, hs):
        t = (tL - 1 - s) if reverse else s
        new = []
        for g in range(G):
            rows = pl.ds(t, B, stride=tL)
            hv = a_ref[g, rows, :] * hs[g] + b_ref[g, rows, :]
            ho_ref[g, rows, :] = hv
            new.append(hv)
        return tuple(new)

    hs = lax.fori_loop(0, tL, step, tuple(hc_ref[g] for g in range(G)))
    for g in range(G):
        hc_ref[g] = hs[g]
        hlast_ref[:, g * bd:(g + 1) * bd] = hs[g]

    if not merge:
        for g in range(G):
            out_ref[:, :, g * bd:(g + 1) * bd] = ho_ref[g].reshape(B, tL, bd)
    else:
        ss = jnp.zeros((B, tL, 1), F32)
        for g in range(G):
            sl = slice(g * bd, (g + 1) * bd)
            hsum = hf_ref[:, :, sl] + ho_ref[g].reshape(B, tL, bd)
            z = hsum * _gelu_tanh(y_ref[:, :, sl].astype(F32))
            ho_ref[g] = z.reshape(B * tL, bd)
            ss = ss + jnp.sum(z * z, axis=-1, keepdims=True)
        inv = lax.rsqrt(ss * (1.0 / (G * bd)) + EPS)
        for g in range(G):
            sl = slice(g * bd, (g + 1) * bd)
            z = ho_ref[g].reshape(B, tL, bd)
            out_ref[:, :, sl] = (z * inv * gn_ref[:, sl]).astype(out_ref.dtype)


def _lru(u3, h0, conv_w, conv_b, w_cat, b_a, b_x, lam, *, reverse, tL, merge_args=None):
    B, L, Dl = u3.shape
    nT = L // tL
    merge = merge_args is not None
    tile = (lambda j: nT - 1 - j) if reverse else (lambda j: j)
    hb = tL // SUBLANES
    n_halo = L // SUBLANES
    row = lambda a: a.reshape(1, Dl)
    in_specs = [
        pl.BlockSpec((B, tL, Dl), lambda j: (0, tile(j), 0)),
        pl.BlockSpec((B, SUBLANES, Dl), lambda j: (0, jnp.maximum(tile(j) * hb - 1, 0), 0)),
        pl.BlockSpec((B, SUBLANES, Dl), lambda j: (0, jnp.minimum((tile(j) + 1) * hb, n_halo - 1), 0)),
        pl.BlockSpec((B, Dl), lambda j: (0, 0)),
        pl.BlockSpec((CONV_W, Dl), lambda j: (0, 0)),
        pl.BlockSpec((1, Dl), lambda j: (0, 0)),
        pl.BlockSpec((LRU_BLOCKS, LRU_BD, 2 * LRU_BD), lambda j: (0, 0, 0)),
        pl.BlockSpec((1, Dl), lambda j: (0, 0)),
        pl.BlockSpec((1, Dl), lambda j: (0, 0)),
        pl.BlockSpec((1, Dl), lambda j: (0, 0)),
    ]
    args = [u3, u3, u3, h0, conv_w, row(conv_b), w_cat, row(b_a), row(b_x), row(lam)]
    if merge:
        hf3, y3, gn = merge_args
        in_specs += [
            pl.BlockSpec((B, tL, Dl), lambda j: (0, tile(j), 0)),
            pl.BlockSpec((B, tL, Dl), lambda j: (0, tile(j), 0)),
            pl.BlockSpec((1, Dl), lambda j: (0, 0)),
        ]
        args += [hf3, y3, row(gn)]
        out_dtype = BF16
    else:
        out_dtype = F32
    out, hlast = pl.pallas_call(
        functools.partial(_lru_kernel, reverse=reverse, merge=merge, B=B, tL=tL, nT=nT),
        grid=(nT,),
        in_specs=in_specs,
        out_specs=[
            pl.BlockSpec((B, tL, Dl), lambda j: (0, tile(j), 0)),
            pl.BlockSpec((B, Dl), lambda j: (0, 0)),
        ],
        out_shape=[jax.ShapeDtypeStruct((B, L, Dl), out_dtype), jax.ShapeDtypeStruct((B, Dl), F32)],
        scratch_shapes=[
            pltpu.VMEM((B, tL + 2 * SUBLANES, Dl), F32),
            pltpu.VMEM((LRU_BLOCKS, B * tL, LRU_BD), F32),
            pltpu.VMEM((LRU_BLOCKS, B * tL, LRU_BD), F32),
            pltpu.VMEM((LRU_BLOCKS, B * tL, LRU_BD), F32),
            pltpu.VMEM((LRU_BLOCKS, B, LRU_BD), F32),
        ],
        compiler_params=_cparams(("arbitrary",)),
        name="lru_bwd_merge" if merge else ("lru_rev" if reverse else "lru_fwd"),
    )(*args)
    return out, hlast


def _out_proj_kernel(ret_ref, lru_ref, x_ref, g1_ref, sc_ref, sh_ref, ng_ref, wo_ref, rw_ref, rb_ref,
                     x1_ref, h2_ref, lg_ref):
    dr = ret_ref.shape[1]
    mix = jnp.dot(ret_ref[...], wo_ref[0:dr, :], preferred_element_type=F32)
    mix = mix + jnp.dot(lru_ref[...], wo_ref[dr:, :], preferred_element_type=F32)
    x1 = x_ref[...] + g1_ref[0] * mix
    x1_ref[...] = x1
    y = x1 * lax.rsqrt(jnp.mean(x1 * x1, axis=-1, keepdims=True) + EPS)
    y = y * ng_ref[...]
    h2 = y * (1.0 + sc_ref[0]) + sh_ref[0]
    h2_ref[...] = h2
    lg_ref[...] = jnp.dot(h2.astype(BF16), rw_ref[...], preferred_element_type=F32) + rb_ref[...]


def _out_proj(ret, lru, x2d, mods3, norm_g, wo_bf, rw_bf, rb, tm, seq_len):
    T, D = x2d.shape
    E = rw_bf.shape[1]
    Dm = wo_bf.shape[0]
    brow = lambda i: (i * tm) // seq_len
    return pl.pallas_call(
        _out_proj_kernel,
        grid=(T // tm,),
        in_specs=[
            pl.BlockSpec((tm, ret.shape[1]), lambda i: (i, 0)),
            pl.BlockSpec((tm, lru.shape[1]), lambda i: (i, 0)),
            pl.BlockSpec((tm, D), lambda i: (i, 0)),
            pl.BlockSpec((1, 1, D), lambda i: (brow(i), 0, 2)),
            pl.BlockSpec((1, 1, D), lambda i: (brow(i), 0, 4)),
            pl.BlockSpec((1, 1, D), lambda i: (brow(i), 0, 3)),
            pl.BlockSpec((1, D), lambda i: (0, 0)),
            pl.BlockSpec((Dm, D), lambda i: (0, 0)),
            pl.BlockSpec((D, E), lambda i: (0, 0)),
            pl.BlockSpec((1, E), lambda i: (0, 0)),
        ],
        out_specs=[
            pl.BlockSpec((tm, D), lambda i: (i, 0)),
            pl.BlockSpec((tm, D), lambda i: (i, 0)),
            pl.BlockSpec((tm, E), lambda i: (i, 0)),
        ],
        out_shape=[
            jax.ShapeDtypeStruct((T, D), F32),
            jax.ShapeDtypeStruct((T, D), F32),
            jax.ShapeDtypeStruct((T, E), F32),
        ],
        compiler_params=_cparams(("arbitrary",)),
        name="out_proj",
    )(ret, lru, x2d, mods3, mods3, mods3, norm_g.reshape(1, D), wo_bf, rw_bf, rb.reshape(1, E))


def _route_kernel(lg_ref, gate_ref, eidx_ref, rank_ref, cnt_ref, carry_ref):
    i = pl.program_id(0)
    tr, E = lg_ref.shape

    @pl.when(i == 0)
    def _():
        carry_ref[...] = jnp.zeros_like(carry_ref)

    work = lg_ref[...]
    lane = lax.broadcasted_iota(jnp.int32, (tr, E), 1).astype(F32)
    vals, idxs, hots = [], [], []
    for _ in range(TOP_K):
        m = jnp.max(work, axis=-1, keepdims=True)
        idx = jnp.min(jnp.where(work == m, lane, float(E)), axis=-1, keepdims=True)
        hot = lane == idx
        vals.append(m)
        idxs.append(idx)
        hots.append(hot)
        work = jnp.where(hot, -jnp.inf, work)
    exps = [jnp.exp(v - vals[0]) for v in vals]
    den = exps[0]
    for e in exps[1:]:
        den = den + e
    onehot = jnp.zeros((tr, E), F32)
    for hot in hots:
        onehot = onehot + hot.astype(F32)
    rr = lax.broadcasted_iota(jnp.int32, (tr, tr), 0)
    cc = lax.broadcasted_iota(jnp.int32, (tr, tr), 1)
    tri = (cc < rr).astype(BF16)
    prefix = jnp.dot(tri, onehot.astype(BF16), preferred_element_type=F32) + carry_ref[...]
    k_lane = lax.broadcasted_iota(jnp.int32, (tr, TOP_K), 1)
    gates = jnp.zeros((tr, TOP_K), F32)
    eidx = jnp.zeros((tr, TOP_K), F32)
    rank = jnp.zeros((tr, TOP_K), F32)
    for k in range(TOP_K):
        rk = jnp.sum(jnp.where(hots[k], prefix, 0.0), axis=-1, keepdims=True)
        gates = jnp.where(k_lane == k, exps[k] / den, gates)
        eidx = jnp.where(k_lane == k, idxs[k], eidx)
        rank = jnp.where(k_lane == k, rk, rank)
    gate_ref[...] = gates
    eidx_ref[...] = eidx.astype(jnp.int32)
    rank_ref[...] = rank.astype(jnp.int32)
    carry_ref[...] = carry_ref[...] + jnp.sum(onehot, axis=0, keepdims=True)
    cnt_ref[...] = carry_ref[...].astype(jnp.int32)


def _route(logits, tr):
    T, E = logits.shape
    return pl.pallas_call(
        _route_kernel,
        grid=(T // tr,),
        in_specs=[pl.BlockSpec((tr, E), lambda i: (i, 0))],
        out_specs=[
            pl.BlockSpec((tr, TOP_K), lambda i: (i, 0)),
            pl.BlockSpec((tr, TOP_K), lambda i: (i, 0)),
            pl.BlockSpec((tr, TOP_K), lambda i: (i, 0)),
            pl.BlockSpec((1, E), lambda i: (0, 0)),
        ],
        out_shape=[
            jax.ShapeDtypeStruct((T, TOP_K), F32),
            jax.ShapeDtypeStruct((T, TOP_K), jnp.int32),
            jax.ShapeDtypeStruct((T, TOP_K), jnp.int32),
            jax.ShapeDtypeStruct((1, E), jnp.int32),
        ],
        scratch_shapes=[pltpu.VMEM((1, E), F32)],
        compiler_params=_cparams(("arbitrary",)),
        name="route",
    )(logits)


def _dispatch_kernel(dest_ref, h_ref, xs_in_ref, xs_ref, sem):
    del xs_in_ref
    td = h_ref.shape[0]
    n = td * TOP_K

    def row_copy(r, dst):
        return pltpu.make_async_copy(h_ref.at[pl.ds(r, 1)], xs_ref.at[pl.ds(dst, 1)], sem)

    def issue(c, carry):
        row_copy(c // TOP_K, dest_ref[0, 0, c]).start()
        return carry

    lax.fori_loop(0, n, issue, 0)

    def drain(c, carry):
        row_copy(0, 0).wait()
        return carry

    lax.fori_loop(0, n, drain, 0)


def _dispatch(h2, dest, xs_init, td):
    T, D = h2.shape
    P = xs_init.shape[0]
    dest3 = dest.reshape(T // td, 1, td * TOP_K)
    return pl.pallas_call(
        _dispatch_kernel,
        grid=(T // td,),
        in_specs=[
            pl.BlockSpec((1, 1, td * TOP_K), lambda i: (i, 0, 0), memory_space=pltpu.SMEM),
            pl.BlockSpec((td, D), lambda i: (i, 0)),
            pl.BlockSpec(memory_space=pl.ANY),
        ],
        out_specs=pl.BlockSpec(memory_space=pl.ANY),
        out_shape=jax.ShapeDtypeStruct((P, D), h2.dtype),
        scratch_shapes=[pltpu.SemaphoreType.DMA(())],
        input_output_aliases={2: 0},
        compiler_params=_cparams(("arbitrary",)),
        name="dispatch",
    )(dest3, h2, xs_init)


def _moe_kernel(be_ref, na_ref, x_ref, wg_ref, bg_ref, wu_ref, bu_ref, wd_ref, bd_ref, y_ref, xb_ref):
    j = pl.program_id(0)
    f = pl.program_id(1)

    @pl.when(jnp.logical_and(j >= na_ref[0], f == 0))
    def _():
        y_ref[...] = jnp.zeros(y_ref.shape, F32)

    @pl.when(j < na_ref[0])
    def _():
        @pl.when(f == 0)
        def _():
            xb_ref[...] = x_ref[...].astype(BF16)
            y_ref[...] = jnp.zeros(y_ref.shape, F32) + bd_ref[...]

        xb = xb_ref[...]
        gl = jnp.dot(xb, wg_ref[...].astype(BF16), preferred_element_type=F32) + bg_ref[...]
        up = jnp.dot(xb, wu_ref[...].astype(BF16), preferred_element_type=F32) + bu_ref[...]
        gl = jnp.minimum(gl, SWIGLU_LIMIT)
        up = jnp.clip(up, -SWIGLU_LIMIT, SWIGLU_LIMIT)
        act = (up + 1.0) * gl * _sigmoid(SWIGLU_ALPHA * gl)
        y_ref[...] += jnp.dot(act.astype(BF16), wd_ref[...].astype(BF16), preferred_element_type=F32)


def _moe(blk_e, n_act, xs, w_gate, b_gate, w_up, b_up, w_down, b_down, tmE, tf):
    P, D = xs.shape
    E, _, Fd = w_gate.shape
    NB = P // tmE
    NF = Fd // tf

    def jj(j, na):
        return jnp.minimum(j, na[0] - 1)

    def ff(j, f, na):
        return jnp.where(j < na[0], f, NF - 1)

    grid_spec = pltpu.PrefetchScalarGridSpec(
        num_scalar_prefetch=2,
        grid=(NB, NF),
        in_specs=[
            pl.BlockSpec((tmE, D), lambda j, f, be, na: (jj(j, na), 0)),
            pl.BlockSpec((None, D, tf), lambda j, f, be, na: (be[jj(j, na)], 0, ff(j, f, na))),
            pl.BlockSpec((None, 1, tf), lambda j, f, be, na: (be[jj(j, na)], 0, ff(j, f, na))),
            pl.BlockSpec((None, D, tf), lambda j, f, be, na: (be[jj(j, na)], 0, ff(j, f, na))),
            pl.BlockSpec((None, 1, tf), lambda j, f, be, na: (be[jj(j, na)], 0, ff(j, f, na))),
            pl.BlockSpec((None, tf, D), lambda j, f, be, na: (be[jj(j, na)], ff(j, f, na), 0)),
            pl.BlockSpec((None, 1, D), lambda j, f, be, na: (be[jj(j, na)], 0, 0)),
        ],
        out_specs=pl.BlockSpec((tmE, D), lambda j, f, be, na: (j, 0)),
        scratch_shapes=[pltpu.VMEM((tmE, D), BF16)],
    )
    return pl.pallas_call(
        _moe_kernel,
        grid_spec=grid_spec,
        out_shape=jax.ShapeDtypeStruct((P, D), F32),
        compiler_params=_cparams(("arbitrary", "arbitrary")),
        name="moe",
    )(blk_e, n_act, xs, w_gate, b_gate.reshape(E, 1, Fd), w_up, b_up.reshape(E, 1, Fd),
      w_down, b_down.reshape(E, 1, D))


def _combine_kernel(dest_ref, x1_ref, gate_ref, g2_ref, fg_ref, ys_ref, o_ref, buf_ref, sem):
    tc = x1_ref.shape[0]
    n = tc * TOP_K

    def row_copy(c, src):
        k = c % TOP_K
        r = c // TOP_K
        return pltpu.make_async_copy(ys_ref.at[pl.ds(src, 1)], buf_ref.at[k, pl.ds(r, 1)], sem)

    def issue(c, carry):
        row_copy(c, dest_ref[0, 0, c]).start()
        return carry

    lax.fori_loop(0, n, issue, 0)

    def drain(c, carry):
        row_copy(0, 0).wait()
        return carry

    lax.fori_loop(0, n, drain, 0)

    gates = gate_ref[...]
    moe = jnp.zeros(x1_ref.shape, F32)
    for k in range(TOP_K):
        moe = moe + buf_ref[k] * gates[:, k:k + 1]
    x2 = x1_ref[...] + g2_ref[0] * moe
    y = x2 * lax.rsqrt(jnp.mean(x2 * x2, axis=-1, keepdims=True) + EPS)
    o_ref[...] = y * fg_ref[...]


def _combine(dest, x1, gates, mods3, final_g, ys, tc, seq_len):
    T, D = x1.shape
    dest3 = dest.reshape(T // tc, 1, tc * TOP_K)
    brow = lambda i: (i * tc) // seq_len
    return pl.pallas_call(
        _combine_kernel,
        grid=(T // tc,),
        in_specs=[
            pl.BlockSpec((1, 1, tc * TOP_K), lambda i: (i, 0, 0), memory_space=pltpu.SMEM),
            pl.BlockSpec((tc, D), lambda i: (i, 0)),
            pl.BlockSpec((tc, TOP_K), lambda i: (i, 0)),
            pl.BlockSpec((1, 1, D), lambda i: (brow(i), 0, 5)),
            pl.BlockSpec((1, D), lambda i: (0, 0)),
            pl.BlockSpec(memory_space=pl.ANY),
        ],
        out_specs=pl.BlockSpec((tc, D), lambda i: (i, 0)),
        out_shape=jax.ShapeDtypeStruct((T, D), F32),
        scratch_shapes=[pltpu.VMEM((TOP_K, tc, D), ys.dtype), pltpu.SemaphoreType.DMA(())],
        compiler_params=_cparams(("arbitrary",)),
        name="combine",
    )(dest3, x1, gates, mods3, final_g.reshape(1, D), ys)


def _pick(n, pref):
    t = min(n, pref)
    while n % t:
        t //= 2
    return t


def _rope_tables(L, hd):
    quarter = hd // 4
    pos = np.arange(L)
    inv = ROPE_THETA ** (-jnp.arange(quarter, dtype=F32) / quarter)
    tabs_c, tabs_s = [], []
    for p in (pos // GRID_W, pos % GRID_W):
        ang = jnp.asarray(p, F32)[:, None] * inv[None, :]
        c, s = jnp.cos(ang), jnp.sin(ang)
        tabs_c += [c, c]
        tabs_s += [-s, s]
    return jnp.concatenate(tabs_c, axis=1), jnp.concatenate(tabs_s, axis=1)


def _layer(x, c, ctx, c_ctx, ada_w, ada_b, norm1_g, norm2_g, w_in, conv_w, conv_b, lru_w_a, lru_b_a,
           lru_w_x, lru_b_x, lru_lambda, ret_norm_g, lru_norm_g, w_out, router_w, router_b,
           w_gate, b_gate, w_up, b_up, w_down, b_down, final_norm_g):
    B, L, D = x.shape
    Lc = ctx.shape[1]
    T = B * L
    Dl = conv_w.shape[-1]
    Dr = (w_in.shape[1] - 2 * Dl) // 4
    hd = Dr // RET_HEADS
    E = router_w.shape[1]
    assert Dr == Dl and hd == 2 * LANES and Dl == LRU_BLOCKS * LRU_BD
    cw = Dr

    R = ((B + 1 + SUBLANES - 1) // SUBLANES) * SUBLANES
    cond = jnp.zeros((R, D), F32).at[:B].set(c).at[B].set(c_ctx)
    mods = _ada(cond, ada_w, ada_b, tn=_pick(6 * D, 1536))
    mods3 = mods.reshape(R, 1, 6 * D)

    w_in_bf = w_in.astype(BF16)
    cos_t, sin_t = _rope_tables(L, hd)
    tm = _pick(L, 512)
    x2d = x.reshape(T, D)
    q, k, v, g, u, y = _in_proj(
        x2d, mods3, lambda i: (i * tm) // L, norm1_g, w_in_bf, cos_t, sin_t,
        ("q_rope", "k_rope", "plain", "plain", "plain", "plain"),
        (BF16, BF16, BF16, F32, F32, F32), tm, L, hd)
    tmc = _pick(Lc, 512)
    w_ctx = jnp.concatenate([w_in_bf[:, cw:2 * cw], w_in_bf[:, 2 * cw:3 * cw], w_in_bf[:, 4 * cw:5 * cw]], axis=1)
    kc, vc, uc = _in_proj(
        ctx.reshape(B * Lc, D), mods3, lambda i: B, norm1_g, w_ctx, cos_t, sin_t,
        ("k_plain", "plain", "plain"), (BF16, BF16, F32), tmc, Lc, hd)

    log_g = jnp.log1p(-(2.0 ** (-5.0 - jnp.arange(RET_HEADS, dtype=F32))))
    ret = _retention(log_g, q, k, v, g, kc, vc, ret_norm_g, B, L, Lc, hd)

    w_cat = jnp.concatenate([lru_w_a, lru_w_x], axis=-1).astype(BF16)
    tL = _pick(L, 32)
    tLc = _pick(Lc, 32)
    h_zero = jnp.zeros((B, Dl), F32)
    lru_args = lambda d: (conv_w, conv_b, w_cat[d], lru_b_a[d], lru_b_x[d], lru_lambda[d])
    uc3 = uc.reshape(B, Lc, Dl)
    u3 = u.reshape(B, L, Dl)
    _, h_f = _lru(uc3, h_zero, *lru_args(0), reverse=False, tL=tLc)
    _, h_b = _lru(uc3, h_zero, *lru_args(1), reverse=True, tL=tLc)
    hf3, _ = _lru(u3, h_f, *lru_args(0), reverse=False, tL=tL)
    lru, _ = _lru(u3, h_b, *lru_args(1), reverse=True, tL=tL,
                  merge_args=(hf3, y.reshape(B, L, Dl), lru_norm_g))
    lru = lru.reshape(T, Dl)

    x1, h2, logits = _out_proj(ret, lru, x2d, mods3, norm2_g, w_out.astype(BF16),
                               router_w.astype(BF16), router_b, _pick(L, 256), L)

    gates, eidx, rank, counts = _route(logits, _pick(T, 1024))
    tmE = _pick(T, 512)
    counts = counts.reshape(E)
    nblk = (counts + tmE - 1) // tmE
    blk_end = jnp.cumsum(nblk)
    pad_start = (blk_end - nblk) * tmE
    NB = (T * TOP_K) // tmE + E
    n_act = blk_end[-1:].astype(jnp.int32)
    blk_e = jnp.clip(jnp.searchsorted(blk_end, jnp.arange(NB), side="right"), 0, E - 1).astype(jnp.int32)
    dest = (jnp.take(pad_start, eidx) + rank).astype(jnp.int32)

    P = NB * tmE
    xs = _dispatch(h2, dest, jnp.zeros((P, D), F32), _pick(T, 256))
    ys = _moe(blk_e, n_act, xs, w_gate, b_gate, w_up, b_up, w_down, b_down, tmE, _pick(w_gate.shape[-1], 256))
    out = _combine(dest, x1, gates, mods3, final_norm_g, ys, _pick(T, 256), L)
    return out.reshape(B, L, D)


def kernel(x, c, ctx, c_ctx, ada_w, ada_b, norm1_g, norm2_g, w_in, conv_w, conv_b, lru_w_a, lru_b_a,
           lru_w_x, lru_b_x, lru_lambda, ret_norm_g, lru_norm_g, w_out, router_w, router_b,
           w_gate, b_gate, w_up, b_up, w_down, b_down, final_norm_g):
    assert ada_w.shape[0] == 1, "single-layer configuration"
    return _layer(x, c, ctx, c_ctx, ada_w[0], ada_b[0], norm1_g[0], norm2_g[0], w_in[0], conv_w[0], conv_b[0],
                  lru_w_a[0], lru_b_a[0], lru_w_x[0], lru_b_x[0], lru_lambda[0], ret_norm_g[0], lru_norm_g[0],
                  w_out[0], router_w[0], router_b[0], w_gate[0], b_gate[0], w_up[0], b_up[0], w_down[0],
                  b_down[0], final_norm_g)
```

```python
import functools

import jax
import jax.numpy as jnp
import numpy as np
from jax import lax
from jax.experimental import pallas as pl
from jax.experimental.pallas import tpu as pltpu

RET_HEADS = 4
LRU_BLOCKS = 8
LRU_BD = 128
CONV_W = 4
GRID_W = 64
TOP_K = 4
LRU_C = 8.0
SWIGLU_LIMIT = 7.0
SWIGLU_ALPHA = 1.702
ROPE_THETA = 10000.0
EPS = 1e-6

LANES = 128
SUBLANES = 8
VMEM_LIMIT = 56 * 1024 * 1024

F32 = jnp.float32
BF16 = jnp.bfloat16


def _cparams(sem):
    return pltpu.CompilerParams(dimension_semantics=sem, vmem_limit_bytes=VMEM_LIMIT)


def _sigmoid(z):
    return 1.0 / (1.0 + jnp.exp(-z))


def _pack_bf16_pairs(v):
    n = v.shape[1] // 2
    lo = lax.bitcast_convert_type(v[:, :n].astype(F32), jnp.uint32)
    hi = lax.bitcast_convert_type(v[:, n:].astype(F32), jnp.uint32)
    return (lo >> 16) | (hi & jnp.uint32(0xFFFF0000))


def _unpack_bf16_pairs(w):
    lo = lax.bitcast_convert_type(w << 16, F32).astype(BF16)
    hi = lax.bitcast_convert_type(w & jnp.uint32(0xFFFF0000), F32).astype(BF16)
    return lo, hi


def _ada_kernel(c_ref, w_ref, b_ref, o_ref):
    c = c_ref[...]
    s = (c * _sigmoid(c)).astype(BF16)
    o_ref[...] = jnp.dot(s, w_ref[...].astype(BF16), preferred_element_type=F32) + b_ref[...]


def _ada(cond, w, b, tn):
    R, D = cond.shape
    N = w.shape[1]
    return pl.pallas_call(
        _ada_kernel,
        grid=(N // tn,),
        in_specs=[
            pl.BlockSpec((R, D), lambda j: (0, 0)),
            pl.BlockSpec((D, tn), lambda j: (0, j)),
            pl.BlockSpec((1, tn), lambda j: (0, j)),
        ],
        out_specs=pl.BlockSpec((R, tn), lambda j: (0, j)),
        out_shape=jax.ShapeDtypeStruct((R, N), F32),
        compiler_params=_cparams(("arbitrary",)),
        name="ada",
    )(cond, w, b.reshape(1, N))


def _in_proj_kernel(*refs, kinds, hd, scale):
    n_out = len(kinds)
    x_ref, sc_ref, sh_ref, g_ref, w_ref, cos_ref, sin_ref = refs[:7]
    outs = refs[7:7 + n_out]
    xn_ref = refs[7 + n_out]
    n = pl.program_id(1)

    @pl.when(n == 0)
    def _():
        x = x_ref[...]
        y = x * lax.rsqrt(jnp.mean(x * x, axis=-1, keepdims=True) + EPS)
        y = y * g_ref[...]
        y = y * (1.0 + sc_ref[0]) + sh_ref[0]
        xn_ref[...] = y.astype(BF16)

    acc = jnp.dot(xn_ref[...], w_ref[...], preferred_element_type=F32)

    def rope(a):
        pieces = []
        n_grp = a.shape[1] // LANES
        per_head = hd // LANES
        for gi in range(n_grp):
            sl = a[:, gi * LANES:(gi + 1) * LANES]
            t = gi % per_head
            c = cos_ref[:, t * LANES:(t + 1) * LANES]
            s = sin_ref[:, t * LANES:(t + 1) * LANES]
            pieces.append(sl * c + pltpu.roll(sl, LANES // 2, 1) * s)
        return jnp.concatenate(pieces, axis=1)

    for idx, kind in enumerate(kinds):
        @pl.when(n == idx)
        def _(idx=idx, kind=kind):
            o = outs[idx]
            if kind == "q_rope":
                o[...] = rope(acc).astype(o.dtype)
            elif kind == "k_rope":
                o[...] = (rope(acc) * scale).astype(o.dtype)
            elif kind == "k_plain":
                o[...] = (acc * scale).astype(o.dtype)
            else:
                o[...] = acc.astype(o.dtype)


def _in_proj(x2d, mods3, mod_row_fn, norm_g, w_bf, cos_t, sin_t, kinds, dtypes, tm, seq_len, hd):
    T, D = x2d.shape
    n_out = len(kinds)
    cw = w_bf.shape[1] // n_out
    tiles_per_seq = seq_len // tm
    in_specs = [
        pl.BlockSpec((tm, D), lambda i, n: (i, 0)),
        pl.BlockSpec((1, 1, D), lambda i, n: (mod_row_fn(i), 0, 1)),
        pl.BlockSpec((1, 1, D), lambda i, n: (mod_row_fn(i), 0, 0)),
        pl.BlockSpec((1, D), lambda i, n: (0, 0)),
        pl.BlockSpec((D, cw), lambda i, n: (0, n)),
        pl.BlockSpec((tm, hd), lambda i, n: (i % tiles_per_seq, 0)),
        pl.BlockSpec((tm, hd), lambda i, n: (i % tiles_per_seq, 0)),
    ]
    out_specs = [pl.BlockSpec((tm, cw), lambda i, n: (i, 0)) for _ in kinds]
    out_shape = [jax.ShapeDtypeStruct((T, cw), dt) for dt in dtypes]
    return pl.pallas_call(
        functools.partial(_in_proj_kernel, kinds=tuple(kinds), hd=hd, scale=float(hd) ** -0.5),
        grid=(T // tm, n_out),
        in_specs=in_specs,
        out_specs=out_specs,
        out_shape=out_shape,
        scratch_shapes=[pltpu.VMEM((tm, D), BF16)],
        compiler_params=_cparams(("arbitrary", "arbitrary")),
        name="in_proj",
    )(x2d, mods3, mods3, norm_g.reshape(1, D), w_bf, cos_t, sin_t)


def _dot_t0(a, b):
    return lax.dot_general(a, b, (((0,), (0,)), ((), ())), preferred_element_type=F32)


def _dot_t1(a, b):
    return lax.dot_general(a, b, (((1,), (1,)), ((), ())), preferred_element_type=F32)


def _ret_kernel(lg_ref, q_ref, k_ref, v_ref, g_ref, kc_ref, vc_ref, gn_ref, o_ref,
                oacc_ref, s_ref, sf_ref, *, C, n_chunks, n_cchunks):
    hd = q_ref.shape[1]
    lg = lg_ref[pl.program_id(1)]
    ri = lax.broadcasted_iota(jnp.int32, (C, C), 0).astype(F32)
    ci = lax.broadcasted_iota(jnp.int32, (C, C), 1).astype(F32)
    dsym = jnp.exp(jnp.abs(ri - ci) * lg)
    pos = lax.broadcasted_iota(jnp.int32, (C, hd), 0).astype(F32)
    qd_f = jnp.exp((pos + 1.0) * lg)
    qd_b = jnp.exp((float(C) - pos) * lg)
    kd_f = jnp.exp((float(C) - 1.0 - pos) * lg)
    kd_b = jnp.exp(pos * lg)
    cd = jnp.exp(jnp.full((1, 1), float(C), F32) * lg)

    def kv_update(s, kk, vv, kd):
        kw = (kk.astype(F32) * kd).astype(BF16)
        upd = _dot_t0(kw, vv)
        return upd if s is None else s * cd + upd

    s_f = None
    for c in range(n_cchunks):
        s_f = kv_update(s_f, kc_ref[c * C:(c + 1) * C, :], vc_ref[c * C:(c + 1) * C, :], kd_f)
    sf_ref[...] = s_f
    s_b = None
    for c in reversed(range(n_cchunks)):
        s_b = kv_update(s_b, kc_ref[c * C:(c + 1) * C, :], vc_ref[c * C:(c + 1) * C, :], kd_b)

    s_ref[...] = s_b

    def bwd(idx, carry):
        n = n_chunks - 1 - idx
        rows = pl.ds(pl.multiple_of(n * C, C), C)
        s = s_ref[...]
        oacc_ref[rows, :] = jnp.dot(q_ref[rows, :], s.astype(BF16), preferred_element_type=F32) * qd_b
        s_ref[...] = kv_update(s, k_ref[rows, :], v_ref[rows, :], kd_b)
        return carry

    lax.fori_loop(0, n_chunks, bwd, 0)

    s_ref[...] = sf_ref[...]

    def fwd(n, carry):
        rows = pl.ds(pl.multiple_of(n * C, C), C)
        s = s_ref[...]
        qn = q_ref[rows, :]
        kn = k_ref[rows, :]
        vn = v_ref[rows, :]
        oacc_ref[rows, :] += jnp.dot(qn, s.astype(BF16), preferred_element_type=F32) * qd_f
        scores = (_dot_t1(qn, kn) * dsym).astype(BF16)
        o = oacc_ref[rows, :] + jnp.dot(scores, vn, preferred_element_type=F32)
        s_ref[...] = kv_update(s, kn, vn, kd_f)
        mu = jnp.mean(o, axis=-1, keepdims=True)
        d = o - mu
        var = jnp.mean(d * d, axis=-1, keepdims=True)
        y = d * lax.rsqrt(var + EPS)
        gate = g_ref[rows, :].astype(F32)
        y = y * gn_ref[...] * (gate * _sigmoid(gate))
        o_ref[rows, :] = y.astype(o_ref.dtype)
        return carry

    lax.fori_loop(0, n_chunks, fwd, 0)


def _retention(log_g, q, k, v, g, kc, vc, gn, B, L, Lc, hd):
    T = q.shape[0]
    H = q.shape[1] // hd
    C = 256
    blk = lambda b, h, *_: (b, h)
    grid_spec = pltpu.PrefetchScalarGridSpec(
        num_scalar_prefetch=1,
        grid=(B, H),
        in_specs=[
            pl.BlockSpec((L, hd), blk),
            pl.BlockSpec((L, hd), blk),
            pl.BlockSpec((L, hd), blk),
            pl.BlockSpec((L, hd), blk),
            pl.BlockSpec((Lc, hd), blk),
            pl.BlockSpec((Lc, hd), blk),
            pl.BlockSpec((1, hd), lambda b, h, *_: (0, h)),
        ],
        out_specs=pl.BlockSpec((L, hd), blk),
        scratch_shapes=[pltpu.VMEM((L, hd), F32), pltpu.VMEM((hd, hd), F32), pltpu.VMEM((hd, hd), F32)],
    )
    return pl.pallas_call(
        functools.partial(_ret_kernel, C=C, n_chunks=L // C, n_cchunks=Lc // C),
        grid_spec=grid_spec,
        out_shape=jax.ShapeDtypeStruct((T, H * hd), BF16),
        compiler_params=_cparams(("arbitrary", "arbitrary")),
        name="retention",
    )(log_g, q, k, v, g, kc, vc, gn.reshape(1, H * hd))


def _gelu_tanh(y):
    return 0.5 * y * (1.0 + jnp.tanh(0.7978845608028654 * (y + 0.044715 * (y * y * y))))


def _lru_kernel(*refs, reverse, merge, B, tL, nT):
    if merge:
        (u_ref, up_ref, un_ref, h0_ref, cw_ref, cb_ref, w_ref, ba_ref, bx_ref, lam_ref,
         hf_ref, y_ref, gn_ref, out_ref, hlast_ref, ext_ref, a_ref, b_ref, ho_ref, hc_ref) = refs
    else:
        (u_ref, up_ref, un_ref, h0_ref, cw_ref, cb_ref, w_ref, ba_ref, bx_ref, lam_ref,
         out_ref, hlast_ref, ext_ref, a_ref, b_ref, ho_ref, hc_ref) = refs
    j = pl.program_id(0)
    tile = (nT - 1 - j) if reverse else j
    G = LRU_BLOCKS
    bd = LRU_BD
    HALO = SUBLANES

    @pl.when(j == 0)
    def _():
        for g in range(G):
            hc_ref[g] = h0_ref[:, g * bd:(g + 1) * bd]

    ext_ref[:, 0:HALO, :] = jnp.where(tile > 0, up_ref[...], 0.0)
    ext_ref[:, HALO:HALO + tL, :] = u_ref[...]
    ext_ref[:, HALO + tL:2 * HALO + tL, :] = jnp.where(tile < nT - 1, un_ref[...], 0.0)
    uc = jnp.zeros((B, tL, G * bd), F32) + cb_ref[...]
    for tap in range(CONV_W):
        off = HALO - 1 + tap
        uc = uc + ext_ref[:, off:off + tL, :] * cw_ref[tap:tap + 1, :]
    uc2 = uc.reshape(B * tL, G * bd)

    lam = lam_ref[...]
    nl = -lam
    softplus = jnp.maximum(nl, 0.0) + jnp.log(1.0 + jnp.exp(-jnp.abs(nl)))
    cfac = -LRU_C * softplus

    for g in range(G):
        sl = slice(g * bd, (g + 1) * bd)
        xg = uc2[:, sl]
        pre = jnp.dot(xg.astype(BF16), w_ref[g], preferred_element_type=F32)
        r = _sigmoid(pre[:, :bd] + ba_ref[:, sl])
        i = _sigmoid(pre[:, bd:] + bx_ref[:, sl])
        log_a = cfac[:, sl] * r
        a = jnp.exp(log_a)
        a_ref[g] = a
        b_ref[g] = jnp.sqrt(1.0 - jnp.exp(2.0 * log_a)) * (i * xg)

    def step(s, hs):
        t = (tL - 1 - s) if reverse else s
        new = []
        for g in range(G):
            rows = pl.ds(t, B, stride=tL)
            hv = a_ref[g, rows, :] * hs[g] + b_ref[g, rows, :]
            ho_ref[g, rows, :] = hv
            new.append(hv)
        return tuple(new)

    hs = lax.fori_loop(0, tL, step, tuple(hc_ref[g] for g in range(G)))
    for g in range(G):
        hc_ref[g] = hs[g]
        hlast_ref[:, g * bd:(g + 1) * bd] = hs[g]

    if not merge:
        for g in range(G):
            out_ref[:, :, g * bd:(g + 1) * bd] = ho_ref[g].reshape(B, tL, bd)
    else:
        ss = jnp.zeros((B, tL, 1), F32)
        for g in range(G):
            sl = slice(g * bd, (g + 1) * bd)
            hsum = hf_ref[:, :, sl] + ho_ref[g].reshape(B, tL, bd)
            z = hsum * _gelu_tanh(y_ref[:, :, sl].astype(F32))
            ho_ref[g] = z.reshape(B * tL, bd)
            ss = ss + jnp.sum(z * z, axis=-1, keepdims=True)
        inv = lax.rsqrt(ss * (1.0 / (G * bd)) + EPS)
        for g in range(G):
            sl = slice(g * bd, (g + 1) * bd)
            z = ho_ref[g].reshape(B, tL, bd)
            out_ref[:, :, sl] = (z * inv * gn_ref[:, sl]).astype(out_ref.dtype)


def _lru(u3, h0, conv_w, conv_b, w_cat, b_a, b_x, lam, *, reverse, tL, merge_args=None):
    B, L, Dl = u3.shape
    nT = L // tL
    merge = merge_args is not None
    tile = (lambda j: nT - 1 - j) if reverse else (lambda j: j)
    hb = tL // SUBLANES
    n_halo = L // SUBLANES
    row = lambda a: a.reshape(1, Dl)
    in_specs = [
        pl.BlockSpec((B, tL, Dl), lambda j: (0, tile(j), 0)),
        pl.BlockSpec((B, SUBLANES, Dl), lambda j: (0, jnp.maximum(tile(j) * hb - 1, 0), 0)),
        pl.BlockSpec((B, SUBLANES, Dl), lambda j: (0, jnp.minimum((tile(j) + 1) * hb, n_halo - 1), 0)),
        pl.BlockSpec((B, Dl), lambda j: (0, 0)),
        pl.BlockSpec((CONV_W, Dl), lambda j: (0, 0)),
        pl.BlockSpec((1, Dl), lambda j: (0, 0)),
        pl.BlockSpec((LRU_BLOCKS, LRU_BD, 2 * LRU_BD), lambda j: (0, 0, 0)),
        pl.BlockSpec((1, Dl), lambda j: (0, 0)),
        pl.BlockSpec((1, Dl), lambda j: (0, 0)),
        pl.BlockSpec((1, Dl), lambda j: (0, 0)),
    ]
    args = [u3, u3, u3, h0, conv_w, row(conv_b), w_cat, row(b_a), row(b_x), row(lam)]
    if merge:
        hf3, y3, gn = merge_args
        in_specs += [
            pl.BlockSpec((B, tL, Dl), lambda j: (0, tile(j), 0)),
            pl.BlockSpec((B, tL, Dl), lambda j: (0, tile(j), 0)),
            pl.BlockSpec((1, Dl), lambda j: (0, 0)),
        ]
        args += [hf3, y3, row(gn)]
        out_dtype = BF16
    else:
        out_dtype = F32
    out, hlast = pl.pallas_call(
        functools.partial(_lru_kernel, reverse=reverse, merge=merge, B=B, tL=tL, nT=nT),
        grid=(nT,),
        in_specs=in_specs,
        out_specs=[
            pl.BlockSpec((B, tL, Dl), lambda j: (0, tile(j), 0)),
            pl.BlockSpec((B, Dl), lambda j: (0, 0)),
        ],
        out_shape=[jax.ShapeDtypeStruct((B, L, Dl), out_dtype), jax.ShapeDtypeStruct((B, Dl), F32)],
        scratch_shapes=[
            pltpu.VMEM((B, tL + 2 * SUBLANES, Dl), F32),
            pltpu.VMEM((LRU_BLOCKS, B * tL, LRU_BD), F32),
            pltpu.VMEM((LRU_BLOCKS, B * tL, LRU_BD), F32),
            pltpu.VMEM((LRU_BLOCKS, B * tL, LRU_BD), F32),
            pltpu.VMEM((LRU_BLOCKS, B, LRU_BD), F32),
        ],
        compiler_params=_cparams(("arbitrary",)),
        name="lru_bwd_merge" if merge else ("lru_rev" if reverse else "lru_fwd"),
    )(*args)
    return out, hlast


def _out_proj_kernel(ret_ref, lru_ref, x_ref, g1_ref, sc_ref, sh_ref, ng_ref, wo_ref, rw_ref, rb_ref,
                     x1_ref, h2_ref, lg_ref):
    dr = ret_ref.shape[1]
    mix = jnp.dot(ret_ref[...], wo_ref[0:dr, :], preferred_element_type=F32)
    mix = mix + jnp.dot(lru_ref[...], wo_ref[dr:, :], preferred_element_type=F32)
    x1 = x_ref[...] + g1_ref[0] * mix
    x1_ref[...] = x1
    y = x1 * lax.rsqrt(jnp.mean(x1 * x1, axis=-1, keepdims=True) + EPS)
    y = y * ng_ref[...]
    h2 = (y * (1.0 + sc_ref[0]) + sh_ref[0]).astype(BF16)
    h2_ref[...] = _pack_bf16_pairs(h2)
    lg_ref[...] = jnp.dot(h2, rw_ref[...], preferred_element_type=F32) + rb_ref[...]


def _out_proj(ret, lru, x2d, mods3, norm_g, wo_bf, rw_bf, rb, tm, seq_len):
    T, D = x2d.shape
    E = rw_bf.shape[1]
    Dm = wo_bf.shape[0]
    brow = lambda i: (i * tm) // seq_len
    return pl.pallas_call(
        _out_proj_kernel,
        grid=(T // tm,),
        in_specs=[
            pl.BlockSpec((tm, ret.shape[1]), lambda i: (i, 0)),
            pl.BlockSpec((tm, lru.shape[1]), lambda i: (i, 0)),
            pl.BlockSpec((tm, D), lambda i: (i, 0)),
            pl.BlockSpec((1, 1, D), lambda i: (brow(i), 0, 2)),
            pl.BlockSpec((1, 1, D), lambda i: (brow(i), 0, 4)),
            pl.BlockSpec((1, 1, D), lambda i: (brow(i), 0, 3)),
            pl.BlockSpec((1, D), lambda i: (0, 0)),
            pl.BlockSpec((Dm, D), lambda i: (0, 0)),
            pl.BlockSpec((D, E), lambda i: (0, 0)),
            pl.BlockSpec((1, E), lambda i: (0, 0)),
        ],
        out_specs=[
            pl.BlockSpec((tm, D), lambda i: (i, 0)),
            pl.BlockSpec((tm, D // 2), lambda i: (i, 0)),
            pl.BlockSpec((tm, E), lambda i: (i, 0)),
        ],
        out_shape=[
            jax.ShapeDtypeStruct((T, D), F32),
            jax.ShapeDtypeStruct((T, D // 2), jnp.uint32),
            jax.ShapeDtypeStruct((T, E), F32),
        ],
        compiler_params=_cparams(("arbitrary",)),
        name="out_proj",
    )(ret, lru, x2d, mods3, mods3, mods3, norm_g.reshape(1, D), wo_bf, rw_bf, rb.reshape(1, E))


def _route_kernel(lg_ref, gate_ref, eidx_ref, rank_ref, cnt_ref, carry_ref):
    i = pl.program_id(0)
    tr, E = lg_ref.shape

    @pl.when(i == 0)
    def _():
        carry_ref[...] = jnp.zeros_like(carry_ref)

    work = lg_ref[...]
    lane = lax.broadcasted_iota(jnp.int32, (tr, E), 1).astype(F32)
    vals, idxs, hots = [], [], []
    for _ in range(TOP_K):
        m = jnp.max(work, axis=-1, keepdims=True)
        idx = jnp.min(jnp.where(work == m, lane, float(E)), axis=-1, keepdims=True)
        hot = lane == idx
        vals.append(m)
        idxs.append(idx)
        hots.append(hot)
        work = jnp.where(hot, -jnp.inf, work)
    exps = [jnp.exp(v - vals[0]) for v in vals]
    den = exps[0]
    for e in exps[1:]:
        den = den + e
    onehot = jnp.zeros((tr, E), F32)
    for hot in hots:
        onehot = onehot + hot.astype(F32)
    rr = lax.broadcasted_iota(jnp.int32, (tr, tr), 0)
    cc = lax.broadcasted_iota(jnp.int32, (tr, tr), 1)
    tri = (cc < rr).astype(BF16)
    prefix = jnp.dot(tri, onehot.astype(BF16), preferred_element_type=F32) + carry_ref[...]
    k_lane = lax.broadcasted_iota(jnp.int32, (tr, TOP_K), 1)
    gates = jnp.zeros((tr, TOP_K), F32)
    eidx = jnp.zeros((tr, TOP_K), F32)
    rank = jnp.zeros((tr, TOP_K), F32)
    for k in range(TOP_K):
        rk = jnp.sum(jnp.where(hots[k], prefix, 0.0), axis=-1, keepdims=True)
        gates = jnp.where(k_lane == k, exps[k] / den, gates)
        eidx = jnp.where(k_lane == k, idxs[k], eidx)
        rank = jnp.where(k_lane == k, rk, rank)
    gate_ref[...] = gates
    eidx_ref[...] = eidx.astype(jnp.int32)
    rank_ref[...] = rank.astype(jnp.int32)
    carry_ref[...] = carry_ref[...] + jnp.sum(onehot, axis=0, keepdims=True)
    cnt_ref[...] = carry_ref[...].astype(jnp.int32)


def _route(logits, tr):
    T, E = logits.shape
    return pl.pallas_call(
        _route_kernel,
        grid=(T // tr,),
        in_specs=[pl.BlockSpec((tr, E), lambda i: (i, 0))],
        out_specs=[
            pl.BlockSpec((tr, TOP_K), lambda i: (i, 0)),
            pl.BlockSpec((tr, TOP_K), lambda i: (i, 0)),
            pl.BlockSpec((tr, TOP_K), lambda i: (i, 0)),
            pl.BlockSpec((1, E), lambda i: (0, 0)),
        ],
        out_shape=[
            jax.ShapeDtypeStruct((T, TOP_K), F32),
            jax.ShapeDtypeStruct((T, TOP_K), jnp.int32),
            jax.ShapeDtypeStruct((T, TOP_K), jnp.int32),
            jax.ShapeDtypeStruct((1, E), jnp.int32),
        ],
        scratch_shapes=[pltpu.VMEM((1, E), F32)],
        compiler_params=_cparams(("arbitrary",)),
        name="route",
    )(logits)


ROW_DMA_UNROLL = 8


def _dispatch_kernel(zrow_ref, nblk_ref, na_ref, dest_ref, h_ref, xs_ref, zero_ref, sem, zsem, *, tmE, NB):
    i = pl.program_id(0)
    td = h_ref.shape[0]
    E = zrow_ref.shape[0]

    def zero_copy(row):
        return pltpu.make_async_copy(zero_ref, xs_ref.at[pl.ds(row, tmE)], zsem)

    @pl.when(i == 0)
    def _():
        zero_ref[...] = jnp.zeros(zero_ref.shape, zero_ref.dtype)
        for e in range(E):
            @pl.when(nblk_ref[e] > 0)
            def _(e=e):
                zero_copy(pl.multiple_of(zrow_ref[e], tmE)).start()

        def ztail(j, carry):
            zero_copy(pl.multiple_of(j * tmE, tmE)).start()
            return carry

        lax.fori_loop(na_ref[0], NB, ztail, 0)
        for e in range(E):
            @pl.when(nblk_ref[e] > 0)
            def _(e=e):
                zero_copy(0).wait()

        def zwait(j, carry):
            zero_copy(0).wait()
            return carry

        lax.fori_loop(na_ref[0], NB, zwait, 0)

    def row_copy(r, dst):
        return pltpu.make_async_copy(h_ref.at[pl.ds(r, 1)], xs_ref.at[pl.ds(dst, 1)], sem)

    def issue(r, carry):
        for k in range(TOP_K):
            row_copy(r, dest_ref[0, 0, r * TOP_K + k]).start()
        return carry

    lax.fori_loop(0, td, issue, 0, unroll=ROW_DMA_UNROLL)

    def drain(r, carry):
        for k in range(TOP_K):
            row_copy(0, 0).wait()
        return carry

    lax.fori_loop(0, td, drain, 0, unroll=ROW_DMA_UNROLL)


def _dispatch(h2p, dest, zrow, nblk, n_act, P, tmE, td):
    T, W = h2p.shape
    dest3 = dest.reshape(T // td, 1, td * TOP_K)
    grid_spec = pltpu.PrefetchScalarGridSpec(
        num_scalar_prefetch=3,
        grid=(T // td,),
        in_specs=[
            pl.BlockSpec((1, 1, td * TOP_K), lambda i, *_: (i, 0, 0), memory_space=pltpu.SMEM),
            pl.BlockSpec((td, W), lambda i, *_: (i, 0)),
        ],
        out_specs=pl.BlockSpec(memory_space=pl.ANY),
        scratch_shapes=[pltpu.VMEM((tmE, W), h2p.dtype), pltpu.SemaphoreType.DMA(()),
                        pltpu.SemaphoreType.DMA(())],
    )
    return pl.pallas_call(
        functools.partial(_dispatch_kernel, tmE=tmE, NB=P // tmE),
        grid_spec=grid_spec,
        out_shape=jax.ShapeDtypeStruct((P, W), h2p.dtype),
        compiler_params=_cparams(("arbitrary",)),
        name="dispatch",
    )(zrow, nblk, n_act, dest3, h2p)


def _moe_kernel(be_ref, na_ref, x_ref, wg_ref, bg_ref, wu_ref, bu_ref, wd_ref, bd_ref, y_ref, xb_ref):
    j = pl.program_id(0)
    f = pl.program_id(1)

    @pl.when(jnp.logical_and(j >= na_ref[0], f == 0))
    def _():
        y_ref[...] = jnp.zeros(y_ref.shape, F32)

    @pl.when(j < na_ref[0])
    def _():
        @pl.when(f == 0)
        def _():
            half = x_ref.shape[1]
            lo, hi = _unpack_bf16_pairs(x_ref[...])
            xb_ref[:, :half] = lo
            xb_ref[:, half:] = hi
            y_ref[...] = jnp.zeros(y_ref.shape, F32) + bd_ref[...]

        xb = xb_ref[...]
        gl = jnp.dot(xb, wg_ref[...].astype(BF16), preferred_element_type=F32) + bg_ref[...]
        up = jnp.dot(xb, wu_ref[...].astype(BF16), preferred_element_type=F32) + bu_ref[...]
        gl = jnp.minimum(gl, SWIGLU_LIMIT)
        up = jnp.clip(up, -SWIGLU_LIMIT, SWIGLU_LIMIT)
        act = (up + 1.0) * gl * _sigmoid(SWIGLU_ALPHA * gl)
        y_ref[...] += jnp.dot(act.astype(BF16), wd_ref[...].astype(BF16), preferred_element_type=F32)


def _moe(blk_e, n_act, xs, w_gate, b_gate, w_up, b_up, w_down, b_down, tmE, tf):
    P, W = xs.shape
    E, D, Fd = w_gate.shape
    NB = P // tmE
    NF = Fd // tf

    def jj(j, na):
        return jnp.minimum(j, na[0] - 1)

    def ff(j, f, na):
        return jnp.where(j < na[0], f, NF - 1)

    grid_spec = pltpu.PrefetchScalarGridSpec(
        num_scalar_prefetch=2,
        grid=(NB, NF),
        in_specs=[
            pl.BlockSpec((tmE, W), lambda j, f, be, na: (jj(j, na), 0)),
            pl.BlockSpec((None, D, tf), lambda j, f, be, na: (be[jj(j, na)], 0, ff(j, f, na))),
            pl.BlockSpec((None, 1, tf), lambda j, f, be, na: (be[jj(j, na)], 0, ff(j, f, na))),
            pl.BlockSpec((None, D, tf), lambda j, f, be, na: (be[jj(j, na)], 0, ff(j, f, na))),
            pl.BlockSpec((None, 1, tf), lambda j, f, be, na: (be[jj(j, na)], 0, ff(j, f, na))),
            pl.BlockSpec((None, tf, D), lambda j, f, be, na: (be[jj(j, na)], ff(j, f, na), 0)),
            pl.BlockSpec((None, 1, D), lambda j, f, be, na: (be[jj(j, na)], 0, 0)),
        ],
        out_specs=pl.BlockSpec((tmE, D), lambda j, f, be, na: (j, 0)),
        scratch_shapes=[pltpu.VMEM((tmE, D), BF16)],
    )
    return pl.pallas_call(
        _moe_kernel,
        grid_spec=grid_spec,
        out_shape=jax.ShapeDtypeStruct((P, D), F32),
        compiler_params=_cparams(("arbitrary", "arbitrary")),
        name="moe",
    )(blk_e, n_act, xs, w_gate, b_gate.reshape(E, 1, Fd), w_up, b_up.reshape(E, 1, Fd),
      w_down, b_down.reshape(E, 1, D))


def _combine_kernel(dest_ref, x1_ref, gate_ref, g2_ref, fg_ref, ys_ref, o_ref, buf_ref, sem):
    tc = x1_ref.shape[0]

    def row_copy(r, k, src):
        return pltpu.make_async_copy(ys_ref.at[pl.ds(src, 1)], buf_ref.at[k, pl.ds(r, 1)], sem)

    def issue(r, carry):
        for k in range(TOP_K):
            row_copy(r, k, dest_ref[0, 0, r * TOP_K + k]).start()
        return carry

    lax.fori_loop(0, tc, issue, 0, unroll=ROW_DMA_UNROLL)

    def drain(r, carry):
        for k in range(TOP_K):
            row_copy(0, 0, 0).wait()
        return carry

    lax.fori_loop(0, tc, drain, 0, unroll=ROW_DMA_UNROLL)

    gates = gate_ref[...]
    moe = jnp.zeros(x1_ref.shape, F32)
    for k in range(TOP_K):
        moe = moe + buf_ref[k] * gates[:, k:k + 1]
    x2 = x1_ref[...] + g2_ref[0] * moe
    y = x2 * lax.rsqrt(jnp.mean(x2 * x2, axis=-1, keepdims=True) + EPS)
    o_ref[...] = y * fg_ref[...]


def _combine(dest, x1, gates, mods3, final_g, ys, tc, seq_len):
    T, D = x1.shape
    dest3 = dest.reshape(T // tc, 1, tc * TOP_K)
    brow = lambda i: (i * tc) // seq_len
    return pl.pallas_call(
        _combine_kernel,
        grid=(T // tc,),
        in_specs=[
            pl.BlockSpec((1, 1, tc * TOP_K), lambda i: (i, 0, 0), memory_space=pltpu.SMEM),
            pl.BlockSpec((tc, D), lambda i: (i, 0)),
            pl.BlockSpec((tc, TOP_K), lambda i: (i, 0)),
            pl.BlockSpec((1, 1, D), lambda i: (brow(i), 0, 5)),
            pl.BlockSpec((1, D), lambda i: (0, 0)),
            pl.BlockSpec(memory_space=pl.ANY),
        ],
        out_specs=pl.BlockSpec((tc, D), lambda i: (i, 0)),
        out_shape=jax.ShapeDtypeStruct((T, D), F32),
        scratch_shapes=[pltpu.VMEM((TOP_K, tc, D), ys.dtype), pltpu.SemaphoreType.DMA(())],
        compiler_params=_cparams(("arbitrary",)),
        name="combine",
    )(dest3, x1, gates, mods3, final_g.reshape(1, D), ys)


def _pick(n, pref):
    t = min(n, pref)
    while n % t:
        t //= 2
    return t


def _rope_tables(L, hd):
    quarter = hd // 4
    pos = np.arange(L)
    inv = ROPE_THETA ** (-jnp.arange(quarter, dtype=F32) / quarter)
    tabs_c, tabs_s = [], []
    for p in (pos // GRID_W, pos % GRID_W):
        ang = jnp.asarray(p, F32)[:, None] * inv[None, :]
        c, s = jnp.cos(ang), jnp.sin(ang)
        tabs_c += [c, c]
        tabs_s += [-s, s]
    return jnp.concatenate(tabs_c, axis=1), jnp.concatenate(tabs_s, axis=1)


def _layer(x, c, ctx, c_ctx, ada_w, ada_b, norm1_g, norm2_g, w_in, conv_w, conv_b, lru_w_a, lru_b_a,
           lru_w_x, lru_b_x, lru_lambda, ret_norm_g, lru_norm_g, w_out, router_w, router_b,
           w_gate, b_gate, w_up, b_up, w_down, b_down, final_norm_g):
    B, L, D = x.shape
    Lc = ctx.shape[1]
    T = B * L
    Dl = conv_w.shape[-1]
    Dr = (w_in.shape[1] - 2 * Dl) // 4
    hd = Dr // RET_HEADS
    E = router_w.shape[1]
    assert Dr == Dl and hd == 2 * LANES and Dl == LRU_BLOCKS * LRU_BD
    cw = Dr

    R = ((B + 1 + SUBLANES - 1) // SUBLANES) * SUBLANES
    cond = jnp.zeros((R, D), F32).at[:B].set(c).at[B].set(c_ctx)
    mods = _ada(cond, ada_w, ada_b, tn=_pick(6 * D, 1536))
    mods3 = mods.reshape(R, 1, 6 * D)

    w_in_bf = w_in.astype(BF16)
    cos_t, sin_t = _rope_tables(L, hd)
    tm = _pick(L, 512)
    x2d = x.reshape(T, D)
    q, k, v, g, u, y = _in_proj(
        x2d, mods3, lambda i: (i * tm) // L, norm1_g, w_in_bf, cos_t, sin_t,
        ("q_rope", "k_rope", "plain", "plain", "plain", "plain"),
        (BF16, BF16, BF16, F32, F32, F32), tm, L, hd)
    tmc = _pick(Lc, 512)
    w_ctx = jnp.concatenate([w_in_bf[:, cw:2 * cw], w_in_bf[:, 2 * cw:3 * cw], w_in_bf[:, 4 * cw:5 * cw]], axis=1)
    kc, vc, uc = _in_proj(
        ctx.reshape(B * Lc, D), mods3, lambda i: B, norm1_g, w_ctx, cos_t, sin_t,
        ("k_plain", "plain", "plain"), (BF16, BF16, F32), tmc, Lc, hd)

    log_g = jnp.log1p(-(2.0 ** (-5.0 - jnp.arange(RET_HEADS, dtype=F32))))
    ret = _retention(log_g, q, k, v, g, kc, vc, ret_norm_g, B, L, Lc, hd)

    w_cat = jnp.concatenate([lru_w_a, lru_w_x], axis=-1).astype(BF16)
    tL = _pick(L, 32)
    tLc = _pick(Lc, 32)
    h_zero = jnp.zeros((B, Dl), F32)
    lru_args = lambda d: (conv_w, conv_b, w_cat[d], lru_b_a[d], lru_b_x[d], lru_lambda[d])
    uc3 = uc.reshape(B, Lc, Dl)
    u3 = u.reshape(B, L, Dl)
    _, h_f = _lru(uc3, h_zero, *lru_args(0), reverse=False, tL=tLc)
    _, h_b = _lru(uc3, h_zero, *lru_args(1), reverse=True, tL=tLc)
    hf3, _ = _lru(u3, h_f, *lru_args(0), reverse=False, tL=tL)
    lru, _ = _lru(u3, h_b, *lru_args(1), reverse=True, tL=tL,
                  merge_args=(hf3, y.reshape(B, L, Dl), lru_norm_g))
    lru = lru.reshape(T, Dl)

    x1, h2, logits = _out_proj(ret, lru, x2d, mods3, norm2_g, w_out.astype(BF16),
                               router_w.astype(BF16), router_b, _pick(L, 256), L)

    gates, eidx, rank, counts = _route(logits, _pick(T, 1024))
    tmE = _pick(T, 1024)
    counts = counts.reshape(E)
    nblk = (counts + tmE - 1) // tmE
    blk_end = jnp.cumsum(nblk)
    pad_start = (blk_end - nblk) * tmE
    NB = (T * TOP_K) // tmE + E
    n_act = blk_end[-1:].astype(jnp.int32)
    blk_e = jnp.minimum(jnp.sum(blk_end[None, :] <= jnp.arange(NB)[:, None], axis=1), E - 1).astype(jnp.int32)
    zrow = (jnp.maximum(blk_end - 1, 0) * tmE).astype(jnp.int32)
    expert_hot = eidx[..., None] == jnp.arange(E, dtype=jnp.int32)
    dest = (jnp.sum(jnp.where(expert_hot, pad_start.astype(jnp.int32), 0), axis=-1) + rank).astype(jnp.int32)

    P = NB * tmE
    xs = _dispatch(h2, dest, zrow, nblk.astype(jnp.int32), n_act, P, tmE, _pick(T, 256))
    ys = _moe(blk_e, n_act, xs, w_gate, b_gate, w_up, b_up, w_down, b_down, tmE, _pick(w_gate.shape[-1], 256))
    out = _combine(dest, x1, gates, mods3, final_norm_g, ys, _pick(T, 256), L)
    return out.reshape(B, L, D)


def kernel(x, c, ctx, c_ctx, ada_w, ada_b, norm1_g, norm2_g, w_in, conv_w, conv_b, lru_w_a, lru_b_a,
           lru_w_x, lru_b_x, lru_lambda, ret_norm_g, lru_norm_g, w_out, router_w, router_b,
           w_gate, b_gate, w_up, b_up, w_down, b_down, final_norm_g):
    assert ada_w.shape[0] == 1, "single-layer configuration"
    return _layer(x, c, ctx, c_ctx, ada_w[0], ada_b[0], norm1_g[0], norm2_g[0], w_in[0], conv_w[0], conv_b[0],
                  lru_w_a[0], lru_b_a[0], lru_w_x[0], lru_b_x[0], lru_lambda[0], ret_norm_g[0], lru_norm_g[0],
                  w_out[0], router_w[0], router_b[0], w_gate[0], b_gate[0], w_up[0], b_up[0], w_down[0],
                  b_down[0], final_norm_g)
```

```python
import functools

import jax
import jax.numpy as jnp
import numpy as np
from jax import lax
from jax.experimental import pallas as pl
from jax.experimental.pallas import tpu as pltpu

RET_HEADS = 4
LRU_BLOCKS = 8
LRU_BD = 128
CONV_W = 4
GRID_W = 64
TOP_K = 4
LRU_C = 8.0
SWIGLU_LIMIT = 7.0
SWIGLU_ALPHA = 1.702
ROPE_THETA = 10000.0
EPS = 1e-6

LANES = 128
SUBLANES = 8
VMEM_LIMIT = 56 * 1024 * 1024

F32 = jnp.float32
BF16 = jnp.bfloat16


def _cparams(sem):
    return pltpu.CompilerParams(dimension_semantics=sem, vmem_limit_bytes=VMEM_LIMIT)


def _sigmoid(z):
    return 0.5 * jnp.tanh(0.5 * z) + 0.5


def _pack_bf16_pairs(v):
    n = v.shape[1] // 2
    lo = lax.bitcast_convert_type(v[:, :n].astype(F32), jnp.uint32)
    hi = lax.bitcast_convert_type(v[:, n:].astype(F32), jnp.uint32)
    return (lo >> 16) | (hi & jnp.uint32(0xFFFF0000))


def _to_row_tiles(v):
    n = v.shape[1] // LANES
    st = jnp.stack([v[:, j * LANES:(j + 1) * LANES] for j in range(n)], axis=0)
    return pltpu.einshape("jtl->tjl", st)


def _from_row_tiles(v3):
    st = pltpu.einshape("tjl->jtl", v3)
    return jnp.concatenate([st[j] for j in range(st.shape[0])], axis=1)


def _unpack_bf16_pairs(w):
    lo = lax.bitcast_convert_type(w << 16, F32).astype(BF16)
    hi = lax.bitcast_convert_type(w & jnp.uint32(0xFFFF0000), F32).astype(BF16)
    return lo, hi


def _ada_kernel(c_ref, w_ref, b_ref, o_ref):
    c = c_ref[...]
    s = (c * _sigmoid(c)).astype(BF16)
    o_ref[...] = jnp.dot(s, w_ref[...].astype(BF16), preferred_element_type=F32) + b_ref[...]


def _ada(cond, w, b, tn):
    R, D = cond.shape
    N = w.shape[1]
    return pl.pallas_call(
        _ada_kernel,
        grid=(N // tn,),
        in_specs=[
            pl.BlockSpec((R, D), lambda j: (0, 0)),
            pl.BlockSpec((D, tn), lambda j: (0, j)),
            pl.BlockSpec((1, tn), lambda j: (0, j)),
        ],
        out_specs=pl.BlockSpec((R, tn), lambda j: (0, j)),
        out_shape=jax.ShapeDtypeStruct((R, N), F32),
        compiler_params=_cparams(("arbitrary",)),
        name="ada",
    )(cond, w, b.reshape(1, N))


def _in_proj_kernel(x_ref, sc_ref, sh_ref, g_ref, w_ref, cos_ref, sin_ref, o, xn_ref, *, kinds, hd, scale):
    n = pl.program_id(1)

    @pl.when(n == 0)
    def _():
        x = x_ref[...]
        y = x * lax.rsqrt(jnp.mean(x * x, axis=-1, keepdims=True) + EPS)
        y = y * g_ref[...]
        y = y * (1.0 + sc_ref[0]) + sh_ref[0]
        xn_ref[...] = y.astype(BF16)

    def matmul():
        return jnp.dot(xn_ref[...], w_ref[...], preferred_element_type=F32)

    def rope(a):
        pieces = []
        n_grp = a.shape[1] // LANES
        per_head = hd // LANES
        for gi in range(n_grp):
            sl = a[:, gi * LANES:(gi + 1) * LANES]
            t = gi % per_head
            c = cos_ref[:, t * LANES:(t + 1) * LANES]
            s = sin_ref[:, t * LANES:(t + 1) * LANES]
            pieces.append(sl * c + pltpu.roll(sl, LANES // 2, 1) * s)
        return jnp.concatenate(pieces, axis=1)

    for kind in dict.fromkeys(kinds):
        cond = functools.reduce(jnp.logical_or, [n == idx for idx, kd in enumerate(kinds) if kd == kind])

        @pl.when(cond)
        def _(kind=kind):
            if kind == "q_rope":
                o[...] = rope(matmul()).astype(o.dtype)
            elif kind == "k_rope":
                o[...] = (rope(matmul()) * scale).astype(o.dtype)
            elif kind == "k_plain":
                o[...] = (matmul() * scale).astype(o.dtype)
            else:
                o[...] = matmul().astype(o.dtype)


def _in_proj(x2d, mods3, mod_row_fn, norm_g, w_bf, cos_t, sin_t, kinds, tm, seq_len, hd):
    T, D = x2d.shape
    n_out = len(kinds)
    cw = w_bf.shape[1] // n_out
    tiles_per_seq = seq_len // tm
    in_specs = [
        pl.BlockSpec((tm, D), lambda i, n: (i, 0)),
        pl.BlockSpec((1, 1, D), lambda i, n: (mod_row_fn(i), 0, 1)),
        pl.BlockSpec((1, 1, D), lambda i, n: (mod_row_fn(i), 0, 0)),
        pl.BlockSpec((1, D), lambda i, n: (0, 0)),
        pl.BlockSpec((D, cw), lambda i, n: (0, n)),
        pl.BlockSpec((tm, hd), lambda i, n: (i % tiles_per_seq, 0)),
        pl.BlockSpec((tm, hd), lambda i, n: (i % tiles_per_seq, 0)),
    ]
    return pl.pallas_call(
        functools.partial(_in_proj_kernel, kinds=tuple(kinds), hd=hd, scale=float(hd) ** -0.5),
        grid=(T // tm, n_out),
        in_specs=in_specs,
        out_specs=pl.BlockSpec((tm, cw), lambda i, n: (i, n)),
        out_shape=jax.ShapeDtypeStruct((T, n_out * cw), BF16),
        scratch_shapes=[pltpu.VMEM((tm, D), BF16)],
        compiler_params=_cparams(("arbitrary", "arbitrary")),
        name="in_proj",
    )(x2d, mods3, mods3, norm_g.reshape(1, D), w_bf, cos_t, sin_t)


def _dot_t0(a, b):
    return lax.dot_general(a, b, (((0,), (0,)), ((), ())), preferred_element_type=F32)


def _dot_t1(a, b):
    return lax.dot_general(a, b, (((1,), (1,)), ((), ())), preferred_element_type=F32)


def _ret_kernel(lg_ref, q_ref, k_ref, v_ref, g_ref, kc_ref, vc_ref, gn_ref, o_ref,
                oacc_ref, s_ref, sf_ref, *, C, n_chunks, n_cchunks):
    hd = q_ref.shape[1]
    lg = lg_ref[pl.program_id(1)]
    ri = lax.broadcasted_iota(jnp.int32, (C, C), 0).astype(F32)
    ci = lax.broadcasted_iota(jnp.int32, (C, C), 1).astype(F32)
    dsym = jnp.exp(jnp.abs(ri - ci) * lg)
    pos = lax.broadcasted_iota(jnp.int32, (C, hd), 0).astype(F32)
    qd_f = jnp.exp((pos + 1.0) * lg)
    qd_b = jnp.exp((float(C) - pos) * lg)
    kd_f = jnp.exp((float(C) - 1.0 - pos) * lg)
    kd_b = jnp.exp(pos * lg)
    cd = jnp.exp(jnp.full((1, 1), float(C), F32) * lg)

    def kv_update(s, kk, vv, kd):
        kw = (kk.astype(F32) * kd).astype(BF16)
        upd = _dot_t0(kw, vv)
        return upd if s is None else s * cd + upd

    s_f = None
    for c in range(n_cchunks):
        s_f = kv_update(s_f, kc_ref[c * C:(c + 1) * C, :], vc_ref[c * C:(c + 1) * C, :], kd_f)
    sf_ref[...] = s_f
    s_b = None
    for c in reversed(range(n_cchunks)):
        s_b = kv_update(s_b, kc_ref[c * C:(c + 1) * C, :], vc_ref[c * C:(c + 1) * C, :], kd_b)

    s_ref[...] = s_b

    def bwd(idx, carry):
        n = n_chunks - 1 - idx
        rows = pl.ds(pl.multiple_of(n * C, C), C)
        s = s_ref[...]
        oacc_ref[rows, :] = jnp.dot(q_ref[rows, :], s.astype(BF16), preferred_element_type=F32) * qd_b
        s_ref[...] = kv_update(s, k_ref[rows, :], v_ref[rows, :], kd_b)
        return carry

    lax.fori_loop(0, n_chunks, bwd, 0)

    s_ref[...] = sf_ref[...]

    def fwd(n, carry):
        rows = pl.ds(pl.multiple_of(n * C, C), C)
        s = s_ref[...]
        qn = q_ref[rows, :]
        kn = k_ref[rows, :]
        vn = v_ref[rows, :]
        oacc_ref[rows, :] += jnp.dot(qn, s.astype(BF16), preferred_element_type=F32) * qd_f
        scores = (_dot_t1(qn, kn) * dsym).astype(BF16)
        o = oacc_ref[rows, :] + jnp.dot(scores, vn, preferred_element_type=F32)
        s_ref[...] = kv_update(s, kn, vn, kd_f)
        mu = jnp.mean(o, axis=-1, keepdims=True)
        d = o - mu
        var = jnp.mean(d * d, axis=-1, keepdims=True)
        y = d * lax.rsqrt(var + EPS)
        gate = g_ref[rows, :].astype(F32)
        y = y * gn_ref[...] * (gate * _sigmoid(gate))
        o_ref[rows, :] = y.astype(o_ref.dtype)
        return carry

    lax.fori_loop(0, n_chunks, fwd, 0)


def _retention(log_g, proj, cols, proj_c, cols_c, gn, B, L, Lc, hd):
    T = proj.shape[0]
    H = gn.shape[0] // hd
    C = 256
    col = lambda chunk: (lambda b, h, *_: (b, chunk * H + h))
    grid_spec = pltpu.PrefetchScalarGridSpec(
        num_scalar_prefetch=1,
        grid=(B, H),
        in_specs=[pl.BlockSpec((L, hd), col(c)) for c in cols]
        + [pl.BlockSpec((Lc, hd), col(c)) for c in cols_c]
        + [pl.BlockSpec((1, hd), lambda b, h, *_: (0, h))],
        out_specs=pl.BlockSpec((L, hd), lambda b, h, *_: (b, h)),
        scratch_shapes=[pltpu.VMEM((L, hd), F32), pltpu.VMEM((hd, hd), F32), pltpu.VMEM((hd, hd), F32)],
    )
    return pl.pallas_call(
        functools.partial(_ret_kernel, C=C, n_chunks=L // C, n_cchunks=Lc // C),
        grid_spec=grid_spec,
        out_shape=jax.ShapeDtypeStruct((T, H * hd), BF16),
        compiler_params=_cparams(("arbitrary", "arbitrary")),
        name="retention",
    )(log_g, proj, proj, proj, proj, proj_c, proj_c, gn.reshape(1, H * hd))


def _gelu_tanh(y):
    return 0.5 * y * (1.0 + jnp.tanh(0.7978845608028654 * (y + 0.044715 * (y * y * y))))


def _lru_kernel(*refs, reverse, merge, B, tL, nT):
    if merge:
        (u_ref, up_ref, un_ref, h0_ref, cw_ref, cb_ref, w_ref, ba_ref, bx_ref, lam_ref,
         hf_ref, y_ref, gn_ref, out_ref, hlast_ref, a_ref, b_ref, ho_ref, hc_ref) = refs
    else:
        (u_ref, up_ref, un_ref, h0_ref, cw_ref, cb_ref, w_ref, ba_ref, bx_ref, lam_ref,
         out_ref, hlast_ref, a_ref, b_ref, ho_ref, hc_ref) = refs
    j = pl.program_id(0)
    tile = (nT - 1 - j) if reverse else j
    G = LRU_BLOCKS
    bd = LRU_BD
    Dl = G * bd
    halo = up_ref.shape[1]

    @pl.when(j == 0)
    def _():
        for g in range(G):
            hc_ref[g] = h0_ref[:, g * bd:(g + 1) * bd]

    cur = pltpu.einshape("btd->tbd", u_ref[...].astype(F32)).reshape(tL * B, Dl)
    prev = jnp.where(tile > 0, up_ref[:, halo - 1, :].astype(F32), 0.0)
    nxt = [jnp.where(tile < nT - 1, un_ref[:, t, :].astype(F32), 0.0) for t in range(CONV_W - 2)]
    ext = jnp.concatenate([prev, cur] + nxt, axis=0)

    uc = jnp.zeros((tL * B, Dl), F32) + cb_ref[...]
    for tap in range(CONV_W):
        uc = uc + ext[tap * B:(tap + tL) * B, :] * cw_ref[tap:tap + 1, :]

    lam = lam_ref[...]
    nl = -lam
    softplus = jnp.maximum(nl, 0.0) + jnp.log(1.0 + jnp.exp(-jnp.abs(nl)))
    cfac = -LRU_C * softplus

    for g in range(G):
        sl = slice(g * bd, (g + 1) * bd)
        xg = uc[:, sl]
        pre = jnp.dot(xg.astype(BF16), w_ref[g], preferred_element_type=F32)
        r = _sigmoid(pre[:, :bd] + ba_ref[:, sl])
        i = _sigmoid(pre[:, bd:] + bx_ref[:, sl])
        a = jnp.exp(cfac[:, sl] * r)
        a_ref[g] = a
        b_ref[g] = jnp.sqrt(1.0 - a * a) * (i * xg)

    def step(s, hs):
        t = (tL - 1 - s) if reverse else s
        rows = pl.ds(pl.multiple_of(t * B, B), B)
        new = []
        for g in range(G):
            hv = a_ref[g, rows, :] * hs[g] + b_ref[g, rows, :]
            ho_ref[g, rows, :] = hv
            new.append(hv)
        return tuple(new)

    hs = lax.fori_loop(0, tL, step, tuple(hc_ref[g] for g in range(G)), unroll=4)
    for g in range(G):
        hc_ref[g] = hs[g]
        hlast_ref[:, g * bd:(g + 1) * bd] = hs[g]

    def batch_major(g):
        return pltpu.einshape("tbd->btd", ho_ref[g].reshape(tL, B, bd))

    if not merge:
        for g in range(G):
            out_ref[:, :, g * bd:(g + 1) * bd] = batch_major(g).astype(out_ref.dtype)
    else:
        ss = jnp.zeros((B, tL, 1), F32)
        for g in range(G):
            sl = slice(g * bd, (g + 1) * bd)
            hsum = hf_ref[:, :, sl].astype(F32) + batch_major(g)
            z = hsum * _gelu_tanh(y_ref[:, :, sl].astype(F32))
            ho_ref[g] = z.reshape(B * tL, bd)
            ss = ss + jnp.sum(z * z, axis=-1, keepdims=True)
        inv = lax.rsqrt(ss * (1.0 / (G * bd)) + EPS)
        for g in range(G):
            sl = slice(g * bd, (g + 1) * bd)
            z = ho_ref[g].reshape(B, tL, bd)
            out_ref[:, :, sl] = (z * inv * gn_ref[:, sl]).astype(out_ref.dtype)


LRU_HALO = 16


def _lru(proj3, ucol, h0, conv_w, conv_b, w_cat, b_a, b_x, lam, *, reverse, tL, merge_args=None):
    B, L, _ = proj3.shape
    Dl = conv_w.shape[-1]
    nT = L // tL
    merge = merge_args is not None
    tile = (lambda j: nT - 1 - j) if reverse else (lambda j: j)
    hb = tL // LRU_HALO
    n_halo = L // LRU_HALO
    row = lambda a: a.reshape(1, Dl)
    in_specs = [
        pl.BlockSpec((B, tL, Dl), lambda j: (0, tile(j), ucol)),
        pl.BlockSpec((B, LRU_HALO, Dl), lambda j: (0, jnp.maximum(tile(j) * hb - 1, 0), ucol)),
        pl.BlockSpec((B, LRU_HALO, Dl), lambda j: (0, jnp.minimum((tile(j) + 1) * hb, n_halo - 1), ucol)),
        pl.BlockSpec((B, Dl), lambda j: (0, 0)),
        pl.BlockSpec((CONV_W, Dl), lambda j: (0, 0)),
        pl.BlockSpec((1, Dl), lambda j: (0, 0)),
        pl.BlockSpec((LRU_BLOCKS, LRU_BD, 2 * LRU_BD), lambda j: (0, 0, 0)),
        pl.BlockSpec((1, Dl), lambda j: (0, 0)),
        pl.BlockSpec((1, Dl), lambda j: (0, 0)),
        pl.BlockSpec((1, Dl), lambda j: (0, 0)),
    ]
    args = [proj3, proj3, proj3, h0, conv_w, row(conv_b), w_cat, row(b_a), row(b_x), row(lam)]
    if merge:
        hf3, ycol, gn = merge_args
        in_specs += [
            pl.BlockSpec((B, tL, Dl), lambda j: (0, tile(j), 0)),
            pl.BlockSpec((B, tL, Dl), lambda j: (0, tile(j), ycol)),
            pl.BlockSpec((1, Dl), lambda j: (0, 0)),
        ]
        args += [hf3, proj3, row(gn)]
        out_dtype = BF16
    else:
        out_dtype = F32
    out, hlast = pl.pallas_call(
        functools.partial(_lru_kernel, reverse=reverse, merge=merge, B=B, tL=tL, nT=nT),
        grid=(nT,),
        in_specs=in_specs,
        out_specs=[
            pl.BlockSpec((B, tL, Dl), lambda j: (0, tile(j), 0)),
            pl.BlockSpec((B, Dl), lambda j: (0, 0)),
        ],
        out_shape=[jax.ShapeDtypeStruct((B, L, Dl), out_dtype), jax.ShapeDtypeStruct((B, Dl), F32)],
        scratch_shapes=[
            pltpu.VMEM((LRU_BLOCKS, B * tL, LRU_BD), F32),
            pltpu.VMEM((LRU_BLOCKS, B * tL, LRU_BD), F32),
            pltpu.VMEM((LRU_BLOCKS, B * tL, LRU_BD), F32),
            pltpu.VMEM((LRU_BLOCKS, B, LRU_BD), F32),
        ],
        compiler_params=_cparams(("arbitrary",)),
        name="lru_bwd_merge" if merge else ("lru_rev" if reverse else "lru_fwd"),
    )(*args)
    return out, hlast


def _out_proj_kernel(ret_ref, lru_ref, x_ref, g1_ref, sc_ref, sh_ref, ng_ref, wo_ref, rw_ref, rb_ref,
                     x1_ref, h2_ref, lg_ref):
    dr = ret_ref.shape[1]
    mix = jnp.dot(ret_ref[...], wo_ref[0:dr, :], preferred_element_type=F32)
    mix = mix + jnp.dot(lru_ref[...], wo_ref[dr:, :], preferred_element_type=F32)
    x1 = x_ref[...] + g1_ref[0] * mix
    x1_ref[...] = x1
    y = x1 * lax.rsqrt(jnp.mean(x1 * x1, axis=-1, keepdims=True) + EPS)
    y = y * ng_ref[...]
    h2 = (y * (1.0 + sc_ref[0]) + sh_ref[0]).astype(BF16)
    h2_ref[...] = _to_row_tiles(_pack_bf16_pairs(h2))
    lg_ref[...] = jnp.dot(h2, rw_ref[...], preferred_element_type=F32) + rb_ref[...]


def _out_proj(ret, lru, x2d, mods3, norm_g, wo_bf, rw_bf, rb, tm, seq_len):
    T, D = x2d.shape
    E = rw_bf.shape[1]
    Dm = wo_bf.shape[0]
    brow = lambda i: (i * tm) // seq_len
    return pl.pallas_call(
        _out_proj_kernel,
        grid=(T // tm,),
        in_specs=[
            pl.BlockSpec((tm, ret.shape[1]), lambda i: (i, 0)),
            pl.BlockSpec((tm, lru.shape[1]), lambda i: (i, 0)),
            pl.BlockSpec((tm, D), lambda i: (i, 0)),
            pl.BlockSpec((1, 1, D), lambda i: (brow(i), 0, 2)),
            pl.BlockSpec((1, 1, D), lambda i: (brow(i), 0, 4)),
            pl.BlockSpec((1, 1, D), lambda i: (brow(i), 0, 3)),
            pl.BlockSpec((1, D), lambda i: (0, 0)),
            pl.BlockSpec((Dm, D), lambda i: (0, 0)),
            pl.BlockSpec((D, E), lambda i: (0, 0)),
            pl.BlockSpec((1, E), lambda i: (0, 0)),
        ],
        out_specs=[
            pl.BlockSpec((tm, D), lambda i: (i, 0)),
            pl.BlockSpec((tm, D // 2 // LANES, LANES), lambda i: (i, 0, 0)),
            pl.BlockSpec((tm, E), lambda i: (i, 0)),
        ],
        out_shape=[
            jax.ShapeDtypeStruct((T, D), F32),
            jax.ShapeDtypeStruct((T, D // 2 // LANES, LANES), jnp.uint32),
            jax.ShapeDtypeStruct((T, E), F32),
        ],
        compiler_params=_cparams(("arbitrary",)),
        name="out_proj",
    )(ret, lru, x2d, mods3, mods3, mods3, norm_g.reshape(1, D), wo_bf, rw_bf, rb.reshape(1, E))


def _route_kernel(lg_ref, gate_ref, eidx_ref, rank_ref, cnt_ref, carry_ref):
    i = pl.program_id(0)
    tr, E = lg_ref.shape

    @pl.when(i == 0)
    def _():
        carry_ref[...] = jnp.zeros_like(carry_ref)

    work = lg_ref[...]
    lane = lax.broadcasted_iota(jnp.int32, (tr, E), 1).astype(F32)
    vals, idxs, hots = [], [], []
    for _ in range(TOP_K):
        m = jnp.max(work, axis=-1, keepdims=True)
        idx = jnp.min(jnp.where(work == m, lane, float(E)), axis=-1, keepdims=True)
        hot = lane == idx
        vals.append(m)
        idxs.append(idx)
        hots.append(hot)
        work = jnp.where(hot, -jnp.inf, work)
    exps = [jnp.exp(v - vals[0]) for v in vals]
    den = exps[0]
    for e in exps[1:]:
        den = den + e
    onehot = jnp.zeros((tr, E), F32)
    for hot in hots:
        onehot = onehot + hot.astype(F32)
    rr = lax.broadcasted_iota(jnp.int32, (tr, tr), 0)
    cc = lax.broadcasted_iota(jnp.int32, (tr, tr), 1)
    tri = (cc < rr).astype(BF16)
    prefix = jnp.dot(tri, onehot.astype(BF16), preferred_element_type=F32) + carry_ref[...]
    k_lane = lax.broadcasted_iota(jnp.int32, (tr, TOP_K), 1)
    gates = jnp.zeros((tr, TOP_K), F32)
    eidx = jnp.zeros((tr, TOP_K), F32)
    rank = jnp.zeros((tr, TOP_K), F32)
    for k in range(TOP_K):
        rk = jnp.sum(jnp.where(hots[k], prefix, 0.0), axis=-1, keepdims=True)
        gates = jnp.where(k_lane == k, exps[k] / den, gates)
        eidx = jnp.where(k_lane == k, idxs[k], eidx)
        rank = jnp.where(k_lane == k, rk, rank)
    gate_ref[...] = gates
    eidx_ref[...] = eidx.astype(jnp.int32)
    rank_ref[...] = rank.astype(jnp.int32)
    carry_ref[...] = carry_ref[...] + jnp.sum(onehot, axis=0, keepdims=True)
    cnt_ref[...] = carry_ref[...].astype(jnp.int32)


def _route(logits, tr):
    T, E = logits.shape
    return pl.pallas_call(
        _route_kernel,
        grid=(T // tr,),
        in_specs=[pl.BlockSpec((tr, E), lambda i: (i, 0))],
        out_specs=[
            pl.BlockSpec((tr, TOP_K), lambda i: (i, 0)),
            pl.BlockSpec((tr, TOP_K), lambda i: (i, 0)),
            pl.BlockSpec((tr, TOP_K), lambda i: (i, 0)),
            pl.BlockSpec((1, E), lambda i: (0, 0)),
        ],
        out_shape=[
            jax.ShapeDtypeStruct((T, TOP_K), F32),
            jax.ShapeDtypeStruct((T, TOP_K), jnp.int32),
            jax.ShapeDtypeStruct((T, TOP_K), jnp.int32),
            jax.ShapeDtypeStruct((1, E), jnp.int32),
        ],
        scratch_shapes=[pltpu.VMEM((1, E), F32)],
        compiler_params=_cparams(("arbitrary",)),
        name="route",
    )(logits)


ROW_DMA_UNROLL = 8


def _dispatch_kernel(zrow_ref, nblk_ref, na_ref, dest_ref, h_ref, xs_ref, zero_ref, sem, zsem, *, tmE, NB):
    i = pl.program_id(0)
    td = h_ref.shape[0]
    E = zrow_ref.shape[0]

    def zero_copy(row):
        return pltpu.make_async_copy(zero_ref, xs_ref.at[pl.ds(row, tmE)], zsem)

    @pl.when(i == 0)
    def _():
        zero_ref[...] = jnp.zeros(zero_ref.shape, zero_ref.dtype)
        for e in range(E):
            @pl.when(nblk_ref[e] > 0)
            def _(e=e):
                zero_copy(pl.multiple_of(zrow_ref[e], tmE)).start()

        def ztail(j, carry):
            zero_copy(pl.multiple_of(j * tmE, tmE)).start()
            return carry

        lax.fori_loop(na_ref[0], NB, ztail, 0)
        for e in range(E):
            @pl.when(nblk_ref[e] > 0)
            def _(e=e):
                zero_copy(0).wait()

        def zwait(j, carry):
            zero_copy(0).wait()
            return carry

        lax.fori_loop(na_ref[0], NB, zwait, 0)

    def row_copy(r, dst):
        return pltpu.make_async_copy(h_ref.at[r], xs_ref.at[dst], sem)

    def issue(r, carry):
        for k in range(TOP_K):
            row_copy(r, dest_ref[0, 0, r * TOP_K + k]).start()
        return carry

    lax.fori_loop(0, td, issue, 0, unroll=ROW_DMA_UNROLL)

    def drain(r, carry):
        for k in range(TOP_K):
            row_copy(0, 0).wait()
        return carry

    lax.fori_loop(0, td, drain, 0, unroll=ROW_DMA_UNROLL)


def _dispatch(h2p, dest, zrow, nblk, n_act, P, tmE, td):
    T, R, _ = h2p.shape
    dest3 = dest.reshape(T // td, 1, td * TOP_K)
    grid_spec = pltpu.PrefetchScalarGridSpec(
        num_scalar_prefetch=3,
        grid=(T // td,),
        in_specs=[
            pl.BlockSpec((1, 1, td * TOP_K), lambda i, *_: (i, 0, 0), memory_space=pltpu.SMEM),
            pl.BlockSpec((td, R, LANES), lambda i, *_: (i, 0, 0)),
        ],
        out_specs=pl.BlockSpec(memory_space=pl.ANY),
        scratch_shapes=[pltpu.VMEM((tmE, R, LANES), h2p.dtype), pltpu.SemaphoreType.DMA(()),
                        pltpu.SemaphoreType.DMA(())],
    )
    return pl.pallas_call(
        functools.partial(_dispatch_kernel, tmE=tmE, NB=P // tmE),
        grid_spec=grid_spec,
        out_shape=jax.ShapeDtypeStruct((P, R, LANES), h2p.dtype),
        compiler_params=_cparams(("arbitrary",)),
        name="dispatch",
    )(zrow, nblk, n_act, dest3, h2p)


MOE_SUB_ROWS = 256


def _moe_kernel(be_ref, na_ref, nv_ref, x_ref, wg_ref, bg_ref, wu_ref, bu_ref, wd_ref, bd_ref, y_ref,
                xb_ref, acc_ref, wgb_ref, wub_ref, wdb_ref):
    j = pl.program_id(0)
    f = pl.program_id(1)
    tmE = y_ref.shape[0]

    @pl.when(jnp.logical_and(j >= na_ref[0], f == 0))
    def _():
        y_ref[...] = jnp.zeros(y_ref.shape, y_ref.dtype)

    @pl.when(j < na_ref[0])
    def _():
        @pl.when(f == 0)
        def _():
            xw = _from_row_tiles(x_ref[...])
            half = xw.shape[1]
            lo, hi = _unpack_bf16_pairs(xw)
            xb_ref[:, :half] = lo
            xb_ref[:, half:] = hi
            acc_ref[...] = jnp.zeros(acc_ref.shape, F32) + bd_ref[...]

        valid = nv_ref[j]

        def expert_mlp(rows, wg, wu, wd):
            xb = xb_ref[rows, :]
            gl = jnp.dot(xb, wg, preferred_element_type=F32) + bg_ref[...]
            up = jnp.dot(xb, wu, preferred_element_type=F32) + bu_ref[...]
            gl = jnp.minimum(gl, SWIGLU_LIMIT)
            up = jnp.clip(up, -SWIGLU_LIMIT, SWIGLU_LIMIT)
            act = (up + 1.0) * gl * _sigmoid(SWIGLU_ALPHA * gl)
            acc_ref[rows, :] += jnp.dot(act.astype(BF16), wd, preferred_element_type=F32)

        @pl.when(valid == tmE)
        def _():
            expert_mlp(slice(0, tmE), wg_ref[...].astype(BF16), wu_ref[...].astype(BF16),
                       wd_ref[...].astype(BF16))

        @pl.when(valid < tmE)
        def _():
            wgb_ref[...] = wg_ref[...].astype(BF16)
            wub_ref[...] = wu_ref[...].astype(BF16)
            wdb_ref[...] = wd_ref[...].astype(BF16)

        for sb in range(tmE // MOE_SUB_ROWS):
            @pl.when(jnp.logical_and(valid < tmE, sb * MOE_SUB_ROWS < valid))
            def _(sb=sb):
                expert_mlp(slice(sb * MOE_SUB_ROWS, (sb + 1) * MOE_SUB_ROWS),
                           wgb_ref[...], wub_ref[...], wdb_ref[...])

        @pl.when(f == pl.num_programs(1) - 1)
        def _():
            y_ref[...] = _to_row_tiles(_pack_bf16_pairs(acc_ref[...].astype(BF16)))


def _moe(blk_e, n_act, n_valid, xs, w_gate, b_gate, w_up, b_up, w_down, b_down, tmE, tf):
    P, R, _ = xs.shape
    E, D, Fd = w_gate.shape
    NB = P // tmE
    NF = Fd // tf

    def jj(j, na):
        return jnp.minimum(j, na[0] - 1)

    def ff(j, f, na):
        return jnp.where(j < na[0], f, NF - 1)

    grid_spec = pltpu.PrefetchScalarGridSpec(
        num_scalar_prefetch=3,
        grid=(NB, NF),
        in_specs=[
            pl.BlockSpec((tmE, R, LANES), lambda j, f, be, na, nv: (jj(j, na), 0, 0)),
            pl.BlockSpec((None, D, tf), lambda j, f, be, na, nv: (be[jj(j, na)], 0, ff(j, f, na))),
            pl.BlockSpec((None, 1, tf), lambda j, f, be, na, nv: (be[jj(j, na)], 0, ff(j, f, na))),
            pl.BlockSpec((None, D, tf), lambda j, f, be, na, nv: (be[jj(j, na)], 0, ff(j, f, na))),
            pl.BlockSpec((None, 1, tf), lambda j, f, be, na, nv: (be[jj(j, na)], 0, ff(j, f, na))),
            pl.BlockSpec((None, tf, D), lambda j, f, be, na, nv: (be[jj(j, na)], ff(j, f, na), 0)),
            pl.BlockSpec((None, 1, D), lambda j, f, be, na, nv: (be[jj(j, na)], 0, 0)),
        ],
        out_specs=pl.BlockSpec((tmE, R, LANES), lambda j, f, be, na, nv: (j, 0, 0)),
        scratch_shapes=[pltpu.VMEM((tmE, D), BF16), pltpu.VMEM((tmE, D), F32), pltpu.VMEM((D, tf), BF16),
                        pltpu.VMEM((D, tf), BF16), pltpu.VMEM((tf, D), BF16)],
    )
    return pl.pallas_call(
        _moe_kernel,
        grid_spec=grid_spec,
        out_shape=jax.ShapeDtypeStruct((P, R, LANES), jnp.uint32),
        compiler_params=_cparams(("arbitrary", "arbitrary")),
        name="moe",
    )(blk_e, n_act, n_valid, xs, w_gate, b_gate.reshape(E, 1, Fd), w_up, b_up.reshape(E, 1, Fd),
      w_down, b_down.reshape(E, 1, D))


def _combine_kernel(dest_ref, x1_ref, gate_ref, g2_ref, fg_ref, ys_ref, o_ref, buf_ref, sem):
    tc = x1_ref.shape[0]

    def row_copy(r, k, src):
        return pltpu.make_async_copy(ys_ref.at[src], buf_ref.at[k, r], sem)

    def issue(r, carry):
        for k in range(TOP_K):
            row_copy(r, k, dest_ref[0, 0, r * TOP_K + k]).start()
        return carry

    lax.fori_loop(0, tc, issue, 0, unroll=ROW_DMA_UNROLL)

    def drain(r, carry):
        for k in range(TOP_K):
            row_copy(0, 0, 0).wait()
        return carry

    lax.fori_loop(0, tc, drain, 0, unroll=ROW_DMA_UNROLL)

    gates = gate_ref[...]
    moe = jnp.zeros(x1_ref.shape, F32)
    for k in range(TOP_K):
        lo, hi = _unpack_bf16_pairs(_from_row_tiles(buf_ref[k]))
        yk = jnp.concatenate([lo.astype(F32), hi.astype(F32)], axis=1)
        moe = moe + yk * gates[:, k:k + 1]
    x2 = x1_ref[...] + g2_ref[0] * moe
    y = x2 * lax.rsqrt(jnp.mean(x2 * x2, axis=-1, keepdims=True) + EPS)
    o_ref[...] = y * fg_ref[...]


def _combine(dest, x1, gates, mods3, final_g, ys, tc, seq_len):
    T, D = x1.shape
    dest3 = dest.reshape(T // tc, 1, tc * TOP_K)
    brow = lambda i: (i * tc) // seq_len
    return pl.pallas_call(
        _combine_kernel,
        grid=(T // tc,),
        in_specs=[
            pl.BlockSpec((1, 1, tc * TOP_K), lambda i: (i, 0, 0), memory_space=pltpu.SMEM),
            pl.BlockSpec((tc, D), lambda i: (i, 0)),
            pl.BlockSpec((tc, TOP_K), lambda i: (i, 0)),
            pl.BlockSpec((1, 1, D), lambda i: (brow(i), 0, 5)),
            pl.BlockSpec((1, D), lambda i: (0, 0)),
            pl.BlockSpec(memory_space=pl.ANY),
        ],
        out_specs=pl.BlockSpec((tc, D), lambda i: (i, 0)),
        out_shape=jax.ShapeDtypeStruct((T, D), F32),
        scratch_shapes=[pltpu.VMEM((TOP_K, tc) + ys.shape[1:], ys.dtype), pltpu.SemaphoreType.DMA(())],
        compiler_params=_cparams(("arbitrary",)),
        name="combine",
    )(dest3, x1, gates, mods3, final_g.reshape(1, D), ys)


def _pick(n, pref):
    t = min(n, pref)
    while n % t:
        t //= 2
    return t


def _rope_tables(L, hd):
    quarter = hd // 4
    pos = np.arange(L)
    inv = ROPE_THETA ** (-jnp.arange(quarter, dtype=F32) / quarter)
    tabs_c, tabs_s = [], []
    for p in (pos // GRID_W, pos % GRID_W):
        ang = jnp.asarray(p, F32)[:, None] * inv[None, :]
        c, s = jnp.cos(ang), jnp.sin(ang)
        tabs_c += [c, c]
        tabs_s += [-s, s]
    return jnp.concatenate(tabs_c, axis=1), jnp.concatenate(tabs_s, axis=1)


def _layer(x, c, ctx, c_ctx, ada_w, ada_b, norm1_g, norm2_g, w_in, conv_w, conv_b, lru_w_a, lru_b_a,
           lru_w_x, lru_b_x, lru_lambda, ret_norm_g, lru_norm_g, w_out, router_w, router_b,
           w_gate, b_gate, w_up, b_up, w_down, b_down, final_norm_g):
    B, L, D = x.shape
    Lc = ctx.shape[1]
    T = B * L
    Dl = conv_w.shape[-1]
    Dr = (w_in.shape[1] - 2 * Dl) // 4
    hd = Dr // RET_HEADS
    E = router_w.shape[1]
    assert Dr == Dl and hd == 2 * LANES and Dl == LRU_BLOCKS * LRU_BD
    cw = Dr

    R = ((B + 1 + SUBLANES - 1) // SUBLANES) * SUBLANES
    cond = jnp.zeros((R, D), F32).at[:B].set(c).at[B].set(c_ctx)
    mods = _ada(cond, ada_w, ada_b, tn=_pick(6 * D, 1536))
    mods3 = mods.reshape(R, 1, 6 * D)

    w_in_bf = w_in.astype(BF16)
    cos_t, sin_t = _rope_tables(L, hd)
    tm = _pick(L, 1024)
    x2d = x.reshape(T, D)
    proj = _in_proj(x2d, mods3, lambda i: (i * tm) // L, norm1_g, w_in_bf, cos_t, sin_t,
                    ("q_rope", "k_rope", "plain", "plain", "plain", "plain"), tm, L, hd)
    tmc = _pick(Lc, 512)
    w_ctx = jnp.concatenate([w_in_bf[:, cw:2 * cw], w_in_bf[:, 2 * cw:3 * cw], w_in_bf[:, 4 * cw:5 * cw]], axis=1)
    proj_c = _in_proj(ctx.reshape(B * Lc, D), mods3, lambda i: B, norm1_g, w_ctx, cos_t, sin_t,
                      ("k_plain", "plain", "plain"), tmc, Lc, hd)

    log_g = jnp.log1p(-(2.0 ** (-5.0 - jnp.arange(RET_HEADS, dtype=F32))))
    ret = _retention(log_g, proj, (0, 1, 2, 3), proj_c, (0, 1), ret_norm_g, B, L, Lc, hd)

    w_cat = jnp.concatenate([lru_w_a, lru_w_x], axis=-1).astype(BF16)
    tL = _pick(L, 32)
    tLc = _pick(Lc, 32)
    h_zero = jnp.zeros((B, Dl), F32)
    lru_args = lambda d: (conv_w, conv_b, w_cat[d], lru_b_a[d], lru_b_x[d], lru_lambda[d])
    proj3 = proj.reshape(B, L, proj.shape[1])
    proj_c3 = proj_c.reshape(B, Lc, proj_c.shape[1])
    _, h_f = _lru(proj_c3, 2, h_zero, *lru_args(0), reverse=False, tL=tLc)
    _, h_b = _lru(proj_c3, 2, h_zero, *lru_args(1), reverse=True, tL=tLc)
    hf3, _ = _lru(proj3, 4, h_f, *lru_args(0), reverse=False, tL=tL)
    lru, _ = _lru(proj3, 4, h_b, *lru_args(1), reverse=True, tL=tL, merge_args=(hf3, 5, lru_norm_g))
    lru = lru.reshape(T, Dl)

    x1, h2, logits = _out_proj(ret, lru, x2d, mods3, norm2_g, w_out.astype(BF16),
                               router_w.astype(BF16), router_b, _pick(L, 256), L)

    gates, eidx, rank, counts = _route(logits, _pick(T, 1024))
    tmE = _pick(T, 1024)
    counts = counts.reshape(E)
    nblk = (counts + tmE - 1) // tmE
    blk_end = jnp.cumsum(nblk)
    pad_start = (blk_end - nblk) * tmE
    NB = (T * TOP_K) // tmE + E
    n_act = blk_end[-1:].astype(jnp.int32)
    blk_e = jnp.minimum(jnp.sum(blk_end[None, :] <= jnp.arange(NB)[:, None], axis=1), E - 1).astype(jnp.int32)
    zrow = (jnp.maximum(blk_end - 1, 0) * tmE).astype(jnp.int32)
    expert_hot = eidx[..., None] == jnp.arange(E, dtype=jnp.int32)
    dest = (jnp.sum(jnp.where(expert_hot, pad_start.astype(jnp.int32), 0), axis=-1) + rank).astype(jnp.int32)

    blk_in_e = jnp.arange(NB) - (blk_end - nblk)[blk_e]
    n_valid = jnp.where(jnp.arange(NB) < n_act[0],
                        jnp.clip(counts[blk_e] - blk_in_e * tmE, 0, tmE), 0).astype(jnp.int32)

    P = NB * tmE
    xs = _dispatch(h2, dest, zrow, nblk.astype(jnp.int32), n_act, P, tmE, _pick(T, 256))
    ys = _moe(blk_e, n_act, n_valid, xs, w_gate, b_gate, w_up, b_up, w_down, b_down, tmE,
              _pick(w_gate.shape[-1], 256))
    out = _combine(dest, x1, gates, mods3, final_norm_g, ys, _pick(T, 256), L)
    return out.reshape(B, L, D)


def kernel(x, c, ctx, c_ctx, ada_w, ada_b, norm1_g, norm2_g, w_in, conv_w, conv_b, lru_w_a, lru_b_a,
           lru_w_x, lru_b_x, lru_lambda, ret_norm_g, lru_norm_g, w_out, router_w, router_b,
           w_gate, b_gate, w_up, b_up, w_down, b_down, final_norm_g):
    assert ada_w.shape[0] == 1, "single-layer configuration"
    return _layer(x, c, ctx, c_ctx, ada_w[0], ada_b[0], norm1_g[0], norm2_g[0], w_in[0], conv_w[0], conv_b[0],
                  lru_w_a[0], lru_b_a[0], lru_w_x[0], lru_b_x[0], lru_lambda[0], ret_norm_g[0], lru_norm_g[0],
                  w_out[0], router_w[0], router_b[0], w_gate[0], b_gate[0], w_up[0], b_up[0], w_down[0],
                  b_down[0], final_norm_g)
```

```python
import functools

import jax
import jax.numpy as jnp
import numpy as np
from jax import lax
from jax.experimental import pallas as pl
from jax.experimental.pallas import tpu as pltpu

RET_HEADS = 4
LRU_BLOCKS = 8
LRU_BD = 128
CONV_W = 4
GRID_W = 64
TOP_K = 4
LRU_C = 8.0
SWIGLU_LIMIT = 7.0
SWIGLU_ALPHA = 1.702
ROPE_THETA = 10000.0
EPS = 1e-6

LANES = 128
SUBLANES = 8
VMEM_LIMIT = 56 * 1024 * 1024

F32 = jnp.float32
BF16 = jnp.bfloat16


def _cparams(sem):
    return pltpu.CompilerParams(dimension_semantics=sem, vmem_limit_bytes=VMEM_LIMIT)


def _sigmoid(z):
    return 0.5 * jnp.tanh(0.5 * z) + 0.5


def _pack_bf16_pairs(v):
    n = v.shape[1] // 2
    lo = lax.bitcast_convert_type(v[:, :n].astype(F32), jnp.uint32)
    hi = lax.bitcast_convert_type(v[:, n:].astype(F32), jnp.uint32)
    return (lo >> 16) | (hi & jnp.uint32(0xFFFF0000))


def _to_row_tiles(v):
    n = v.shape[1] // LANES
    st = jnp.stack([v[:, j * LANES:(j + 1) * LANES] for j in range(n)], axis=0)
    return pltpu.einshape("jtl->tjl", st)


def _from_row_tiles(v3):
    st = pltpu.einshape("tjl->jtl", v3)
    return jnp.concatenate([st[j] for j in range(st.shape[0])], axis=1)


def _unpack_bf16_pairs(w):
    lo = lax.bitcast_convert_type(w << 16, F32)
    hi = lax.bitcast_convert_type(w & jnp.uint32(0xFFFF0000), F32)
    return lo, hi


def _ada_kernel(c_ref, w_ref, b_ref, o_ref):
    c = c_ref[...]
    s = (c * _sigmoid(c)).astype(BF16)
    o_ref[...] = jnp.dot(s, w_ref[...].astype(BF16), preferred_element_type=F32) + b_ref[...]


def _ada(cond, w, b, tn):
    R, D = cond.shape
    N = w.shape[1]
    return pl.pallas_call(
        _ada_kernel,
        grid=(N // tn,),
        in_specs=[
            pl.BlockSpec((R, D), lambda j: (0, 0)),
            pl.BlockSpec((D, tn), lambda j: (0, j)),
            pl.BlockSpec((1, tn), lambda j: (0, j)),
        ],
        out_specs=pl.BlockSpec((R, tn), lambda j: (0, j)),
        out_shape=jax.ShapeDtypeStruct((R, N), F32),
        compiler_params=_cparams(("arbitrary",)),
        name="ada",
    )(cond, w, b.reshape(1, N))


def _in_proj_kernel(x_ref, sc_ref, sh_ref, g_ref, w_ref, cos_ref, sin_ref, o, xn_ref, *, kinds, hd, scale):
    n = pl.program_id(1)

    @pl.when(n == 0)
    def _():
        x = x_ref[...]
        y = x * lax.rsqrt(jnp.mean(x * x, axis=-1, keepdims=True) + EPS)
        y = y * g_ref[...]
        y = y * (1.0 + sc_ref[0]) + sh_ref[0]
        xn_ref[...] = y.astype(BF16)

    def matmul():
        return jnp.dot(xn_ref[...], w_ref[...], preferred_element_type=F32)

    def rope(a):
        pieces = []
        n_grp = a.shape[1] // LANES
        per_head = hd // LANES
        for gi in range(n_grp):
            sl = a[:, gi * LANES:(gi + 1) * LANES]
            t = gi % per_head
            c = cos_ref[:, t * LANES:(t + 1) * LANES]
            s = sin_ref[:, t * LANES:(t + 1) * LANES]
            pieces.append(sl * c + pltpu.roll(sl, LANES // 2, 1) * s)
        return jnp.concatenate(pieces, axis=1)

    for kind in dict.fromkeys(kinds):
        cond = functools.reduce(jnp.logical_or, [n == idx for idx, kd in enumerate(kinds) if kd == kind])

        @pl.when(cond)
        def _(kind=kind):
            if kind == "q_rope":
                o[...] = rope(matmul()).astype(o.dtype)
            elif kind == "k_rope":
                o[...] = (rope(matmul()) * scale).astype(o.dtype)
            elif kind == "k_plain":
                o[...] = (matmul() * scale).astype(o.dtype)
            else:
                o[...] = matmul().astype(o.dtype)


def _in_proj(x2d, mods3, mod_row_fn, norm_g, w_bf, cos_t, sin_t, kinds, tm, seq_len, hd):
    T, D = x2d.shape
    n_out = len(kinds)
    cw = w_bf.shape[1] // n_out
    tiles_per_seq = seq_len // tm
    in_specs = [
        pl.BlockSpec((tm, D), lambda i, n: (i, 0)),
        pl.BlockSpec((1, 1, D), lambda i, n: (mod_row_fn(i), 0, 1)),
        pl.BlockSpec((1, 1, D), lambda i, n: (mod_row_fn(i), 0, 0)),
        pl.BlockSpec((1, D), lambda i, n: (0, 0)),
        pl.BlockSpec((D, cw), lambda i, n: (0, n)),
        pl.BlockSpec((tm, hd), lambda i, n: (i % tiles_per_seq, 0)),
        pl.BlockSpec((tm, hd), lambda i, n: (i % tiles_per_seq, 0)),
    ]
    return pl.pallas_call(
        functools.partial(_in_proj_kernel, kinds=tuple(kinds), hd=hd, scale=float(hd) ** -0.5),
        grid=(T // tm, n_out),
        in_specs=in_specs,
        out_specs=pl.BlockSpec((tm, cw), lambda i, n: (i, n)),
        out_shape=jax.ShapeDtypeStruct((T, n_out * cw), BF16),
        scratch_shapes=[pltpu.VMEM((tm, D), BF16)],
        compiler_params=_cparams(("arbitrary", "arbitrary")),
        name="in_proj",
    )(x2d, mods3, mods3, norm_g.reshape(1, D), w_bf, cos_t, sin_t)


def _dot_t0(a, b):
    return lax.dot_general(a, b, (((0,), (0,)), ((), ())), preferred_element_type=F32)


def _dot_t1(a, b):
    return lax.dot_general(a, b, (((1,), (1,)), ((), ())), preferred_element_type=F32)


RET_CHUNK = 256
RET_UNROLL = 4


def _ret_kernel(lg_ref, q_ref, k_ref, v_ref, g_ref, kc_ref, vc_ref, gn_ref, o_ref,
                oacc_ref, s_ref, sf_ref, dec_ref, *, C, n_chunks, n_cchunks):
    hd = q_ref.shape[1]
    lg = lg_ref[pl.program_id(0)]

    @pl.when(pl.program_id(1) == 0)
    def _():
        ri = lax.broadcasted_iota(jnp.int32, (C, C), 0).astype(F32)
        ci = lax.broadcasted_iota(jnp.int32, (C, C), 1).astype(F32)
        pos = lax.broadcasted_iota(jnp.int32, (C, hd), 0).astype(F32)
        dec_ref[0] = jnp.exp(jnp.abs(ri - ci) * lg)
        dec_ref[1] = jnp.exp((pos + 1.0) * lg)
        dec_ref[2] = jnp.exp((float(C) - pos) * lg)
        dec_ref[3] = jnp.exp((float(C) - 1.0 - pos) * lg)
        dec_ref[4] = jnp.exp(pos * lg)

    DSYM, QD_F, QD_B, KD_F, KD_B = range(5)
    cd = jnp.exp(jnp.full((1, 1), float(C), F32) * lg)

    def kv_update(s, kk, vv, kd_slot):
        kw = (kk.astype(F32) * dec_ref[kd_slot]).astype(BF16)
        upd = _dot_t0(kw, vv)
        return upd if s is None else s * cd + upd

    s_f = None
    for c in range(n_cchunks):
        s_f = kv_update(s_f, kc_ref[c * C:(c + 1) * C, :], vc_ref[c * C:(c + 1) * C, :], KD_F)
    sf_ref[...] = s_f
    s_b = None
    for c in reversed(range(n_cchunks)):
        s_b = kv_update(s_b, kc_ref[c * C:(c + 1) * C, :], vc_ref[c * C:(c + 1) * C, :], KD_B)

    s_ref[...] = s_b

    def bwd(idx, carry):
        n = n_chunks - 1 - idx
        rows = pl.ds(pl.multiple_of(n * C, C), C)
        s = s_ref[...]
        oacc_ref[rows, :] = jnp.dot(q_ref[rows, :], s.astype(BF16), preferred_element_type=F32) * dec_ref[QD_B]
        s_ref[...] = kv_update(s, k_ref[rows, :], v_ref[rows, :], KD_B)
        return carry

    lax.fori_loop(0, n_chunks, bwd, 0, unroll=RET_UNROLL)

    s_ref[...] = sf_ref[...]

    def fwd(n, carry):
        rows = pl.ds(pl.multiple_of(n * C, C), C)
        s = s_ref[...]
        qn = q_ref[rows, :]
        kn = k_ref[rows, :]
        vn = v_ref[rows, :]
        oacc_ref[rows, :] += jnp.dot(qn, s.astype(BF16), preferred_element_type=F32) * dec_ref[QD_F]
        scores = (_dot_t1(qn, kn) * dec_ref[DSYM]).astype(BF16)
        o = oacc_ref[rows, :] + jnp.dot(scores, vn, preferred_element_type=F32)
        s_ref[...] = kv_update(s, kn, vn, KD_F)
        mu = jnp.mean(o, axis=-1, keepdims=True)
        d = o - mu
        var = jnp.mean(d * d, axis=-1, keepdims=True)
        y = d * lax.rsqrt(var + EPS)
        gate = g_ref[rows, :].astype(F32)
        y = y * gn_ref[...] * (gate * _sigmoid(gate))
        o_ref[rows, :] = y.astype(o_ref.dtype)
        return carry

    lax.fori_loop(0, n_chunks, fwd, 0, unroll=RET_UNROLL)


def _retention(log_g, proj, cols, proj_c, cols_c, gn, B, L, Lc, hd):
    T = proj.shape[0]
    H = gn.shape[0] // hd
    C = RET_CHUNK
    assert C == hd
    col = lambda chunk: (lambda h, b, *_: (b, chunk * H + h))
    grid_spec = pltpu.PrefetchScalarGridSpec(
        num_scalar_prefetch=1,
        grid=(H, B),
        in_specs=[pl.BlockSpec((L, hd), col(c)) for c in cols]
        + [pl.BlockSpec((Lc, hd), col(c)) for c in cols_c]
        + [pl.BlockSpec((1, hd), lambda h, b, *_: (0, h))],
        out_specs=pl.BlockSpec((L, hd), lambda h, b, *_: (b, h)),
        scratch_shapes=[pltpu.VMEM((L, hd), F32), pltpu.VMEM((hd, hd), F32), pltpu.VMEM((hd, hd), F32),
                        pltpu.VMEM((5, C, hd), F32)],
    )
    return pl.pallas_call(
        functools.partial(_ret_kernel, C=C, n_chunks=L // C, n_cchunks=Lc // C),
        grid_spec=grid_spec,
        out_shape=jax.ShapeDtypeStruct((T, H * hd), BF16),
        compiler_params=_cparams(("arbitrary", "arbitrary")),
        name="retention",
    )(log_g, proj, proj, proj, proj, proj_c, proj_c, gn.reshape(1, H * hd))


def _gelu_tanh(y):
    return 0.5 * y * (1.0 + jnp.tanh(0.7978845608028654 * (y + 0.044715 * (y * y * y))))


def _lru_kernel(*refs, reverse, merge, B, tL, nT):
    if merge:
        (u_ref, up_ref, un_ref, h0_ref, cw_ref, cb_ref, w_ref, ba_ref, bx_ref, lam_ref,
         hf_ref, y_ref, gn_ref, out_ref, hlast_ref, a_ref, b_ref, ho_ref, hc_ref) = refs
    else:
        (u_ref, up_ref, un_ref, h0_ref, cw_ref, cb_ref, w_ref, ba_ref, bx_ref, lam_ref,
         out_ref, hlast_ref, a_ref, b_ref, ho_ref, hc_ref) = refs
    j = pl.program_id(0)
    tile = (nT - 1 - j) if reverse else j
    G = LRU_BLOCKS
    bd = LRU_BD
    Dl = G * bd
    halo = up_ref.shape[1]

    @pl.when(j == 0)
    def _():
        for g in range(G):
            hc_ref[g] = h0_ref[:, g * bd:(g + 1) * bd]

    cur = pltpu.einshape("btd->tbd", u_ref[...].astype(F32)).reshape(tL * B, Dl)
    prev = jnp.where(tile > 0, up_ref[:, halo - 1, :].astype(F32), 0.0)
    nxt = [jnp.where(tile < nT - 1, un_ref[:, t, :].astype(F32), 0.0) for t in range(CONV_W - 2)]
    ext = jnp.concatenate([prev, cur] + nxt, axis=0)

    uc = jnp.zeros((tL * B, Dl), F32) + cb_ref[...]
    for tap in range(CONV_W):
        uc = uc + ext[tap * B:(tap + tL) * B, :] * cw_ref[tap:tap + 1, :]

    lam = lam_ref[...]
    nl = -lam
    softplus = jnp.maximum(nl, 0.0) + jnp.log(1.0 + jnp.exp(-jnp.abs(nl)))
    cfac = -LRU_C * softplus

    for g in range(G):
        sl = slice(g * bd, (g + 1) * bd)
        xg = uc[:, sl]
        pre = jnp.dot(xg.astype(BF16), w_ref[g], preferred_element_type=F32)
        r = _sigmoid(pre[:, :bd] + ba_ref[:, sl])
        i = _sigmoid(pre[:, bd:] + bx_ref[:, sl])
        a = jnp.exp(cfac[:, sl] * r)
        a_ref[g] = a
        b_ref[g] = jnp.sqrt(1.0 - a * a) * (i * xg)

    def step(s, hs):
        t = (tL - 1 - s) if reverse else s
        rows = pl.ds(pl.multiple_of(t * B, B), B)
        new = []
        for g in range(G):
            hv = a_ref[g, rows, :] * hs[g] + b_ref[g, rows, :]
            ho_ref[g, rows, :] = hv
            new.append(hv)
        return tuple(new)

    hs = lax.fori_loop(0, tL, step, tuple(hc_ref[g] for g in range(G)), unroll=4)
    for g in range(G):
        hc_ref[g] = hs[g]
        hlast_ref[:, g * bd:(g + 1) * bd] = hs[g]

    def batch_major(g):
        return pltpu.einshape("tbd->btd", ho_ref[g].reshape(tL, B, bd))

    if not merge:
        for g in range(G):
            out_ref[:, :, g * bd:(g + 1) * bd] = batch_major(g).astype(out_ref.dtype)
    else:
        ss = jnp.zeros((B, tL, 1), F32)
        for g in range(G):
            sl = slice(g * bd, (g + 1) * bd)
            hsum = hf_ref[:, :, sl].astype(F32) + batch_major(g)
            z = hsum * _gelu_tanh(y_ref[:, :, sl].astype(F32))
            ho_ref[g] = z.reshape(B * tL, bd)
            ss = ss + jnp.sum(z * z, axis=-1, keepdims=True)
        inv = lax.rsqrt(ss * (1.0 / (G * bd)) + EPS)
        for g in range(G):
            sl = slice(g * bd, (g + 1) * bd)
            z = ho_ref[g].reshape(B, tL, bd)
            out_ref[:, :, sl] = (z * inv * gn_ref[:, sl]).astype(out_ref.dtype)


LRU_HALO = 16


def _lru(proj3, ucol, h0, conv_w, conv_b, w_cat, b_a, b_x, lam, *, reverse, tL, merge_args=None):
    B, L, _ = proj3.shape
    Dl = conv_w.shape[-1]
    nT = L // tL
    merge = merge_args is not None
    tile = (lambda j: nT - 1 - j) if reverse else (lambda j: j)
    hb = tL // LRU_HALO
    n_halo = L // LRU_HALO
    row = lambda a: a.reshape(1, Dl)
    in_specs = [
        pl.BlockSpec((B, tL, Dl), lambda j: (0, tile(j), ucol)),
        pl.BlockSpec((B, LRU_HALO, Dl), lambda j: (0, jnp.maximum(tile(j) * hb - 1, 0), ucol)),
        pl.BlockSpec((B, LRU_HALO, Dl), lambda j: (0, jnp.minimum((tile(j) + 1) * hb, n_halo - 1), ucol)),
        pl.BlockSpec((B, Dl), lambda j: (0, 0)),
        pl.BlockSpec((CONV_W, Dl), lambda j: (0, 0)),
        pl.BlockSpec((1, Dl), lambda j: (0, 0)),
        pl.BlockSpec((LRU_BLOCKS, LRU_BD, 2 * LRU_BD), lambda j: (0, 0, 0)),
        pl.BlockSpec((1, Dl), lambda j: (0, 0)),
        pl.BlockSpec((1, Dl), lambda j: (0, 0)),
        pl.BlockSpec((1, Dl), lambda j: (0, 0)),
    ]
    args = [proj3, proj3, proj3, h0, conv_w, row(conv_b), w_cat, row(b_a), row(b_x), row(lam)]
    if merge:
        hf3, ycol, gn = merge_args
        in_specs += [
            pl.BlockSpec((B, tL, Dl), lambda j: (0, tile(j), 0)),
            pl.BlockSpec((B, tL, Dl), lambda j: (0, tile(j), ycol)),
            pl.BlockSpec((1, Dl), lambda j: (0, 0)),
        ]
        args += [hf3, proj3, row(gn)]
        out_dtype = BF16
    else:
        out_dtype = F32
    out, hlast = pl.pallas_call(
        functools.partial(_lru_kernel, reverse=reverse, merge=merge, B=B, tL=tL, nT=nT),
        grid=(nT,),
        in_specs=in_specs,
        out_specs=[
            pl.BlockSpec((B, tL, Dl), lambda j: (0, tile(j), 0)),
            pl.BlockSpec((B, Dl), lambda j: (0, 0)),
        ],
        out_shape=[jax.ShapeDtypeStruct((B, L, Dl), out_dtype), jax.ShapeDtypeStruct((B, Dl), F32)],
        scratch_shapes=[
            pltpu.VMEM((LRU_BLOCKS, B * tL, LRU_BD), F32),
            pltpu.VMEM((LRU_BLOCKS, B * tL, LRU_BD), F32),
            pltpu.VMEM((LRU_BLOCKS, B * tL, LRU_BD), F32),
            pltpu.VMEM((LRU_BLOCKS, B, LRU_BD), F32),
        ],
        compiler_params=_cparams(("arbitrary",)),
        name="lru_bwd_merge" if merge else ("lru_rev" if reverse else "lru_fwd"),
    )(*args)
    return out, hlast


def _out_proj_kernel(ret_ref, lru_ref, x_ref, g1_ref, sc_ref, sh_ref, ng_ref, wo_ref, rw_ref, rb_ref,
                     x1_ref, h2_ref, lg_ref):
    dr = ret_ref.shape[1]
    mix = jnp.dot(ret_ref[...], wo_ref[0:dr, :], preferred_element_type=F32)
    mix = mix + jnp.dot(lru_ref[...], wo_ref[dr:, :], preferred_element_type=F32)
    x1 = x_ref[...] + g1_ref[0] * mix
    x1_ref[...] = x1
    y = x1 * lax.rsqrt(jnp.mean(x1 * x1, axis=-1, keepdims=True) + EPS)
    y = y * ng_ref[...]
    h2 = (y * (1.0 + sc_ref[0]) + sh_ref[0]).astype(BF16)
    h2_ref[...] = _to_row_tiles(_pack_bf16_pairs(h2))
    lg_ref[...] = jnp.dot(h2, rw_ref[...], preferred_element_type=F32) + rb_ref[...]


def _out_proj(ret, lru, x2d, mods3, norm_g, wo_bf, rw_bf, rb, tm, seq_len):
    T, D = x2d.shape
    E = rw_bf.shape[1]
    Dm = wo_bf.shape[0]
    brow = lambda i: (i * tm) // seq_len
    return pl.pallas_call(
        _out_proj_kernel,
        grid=(T // tm,),
        in_specs=[
            pl.BlockSpec((tm, ret.shape[1]), lambda i: (i, 0)),
            pl.BlockSpec((tm, lru.shape[1]), lambda i: (i, 0)),
            pl.BlockSpec((tm, D), lambda i: (i, 0)),
            pl.BlockSpec((1, 1, D), lambda i: (brow(i), 0, 2)),
            pl.BlockSpec((1, 1, D), lambda i: (brow(i), 0, 4)),
            pl.BlockSpec((1, 1, D), lambda i: (brow(i), 0, 3)),
            pl.BlockSpec((1, D), lambda i: (0, 0)),
            pl.BlockSpec((Dm, D), lambda i: (0, 0)),
            pl.BlockSpec((D, E), lambda i: (0, 0)),
            pl.BlockSpec((1, E), lambda i: (0, 0)),
        ],
        out_specs=[
            pl.BlockSpec((tm, D), lambda i: (i, 0)),
            pl.BlockSpec((tm, D // 2 // LANES, LANES), lambda i: (i, 0, 0)),
            pl.BlockSpec((tm, E), lambda i: (i, 0)),
        ],
        out_shape=[
            jax.ShapeDtypeStruct((T, D), F32),
            jax.ShapeDtypeStruct((T, D // 2 // LANES, LANES), jnp.uint32),
            jax.ShapeDtypeStruct((T, E), F32),
        ],
        compiler_params=_cparams(("arbitrary",)),
        name="out_proj",
    )(ret, lru, x2d, mods3, mods3, mods3, norm_g.reshape(1, D), wo_bf, rw_bf, rb.reshape(1, E))


def _route_kernel(lg_ref, gate_ref, eidx_ref, rank_ref, cnt_ref, carry_ref):
    i = pl.program_id(0)
    tr, E = lg_ref.shape

    @pl.when(i == 0)
    def _():
        carry_ref[...] = jnp.zeros_like(carry_ref)

    work = lg_ref[...]
    lane = lax.broadcasted_iota(jnp.int32, (tr, E), 1).astype(F32)
    vals, idxs, hots = [], [], []
    for _ in range(TOP_K):
        m = jnp.max(work, axis=-1, keepdims=True)
        idx = jnp.min(jnp.where(work == m, lane, float(E)), axis=-1, keepdims=True)
        hot = lane == idx
        vals.append(m)
        idxs.append(idx)
        hots.append(hot)
        work = jnp.where(hot, -jnp.inf, work)
    exps = [jnp.exp(v - vals[0]) for v in vals]
    den = exps[0]
    for e in exps[1:]:
        den = den + e
    onehot = jnp.zeros((tr, E), F32)
    for hot in hots:
        onehot = onehot + hot.astype(F32)
    rr = lax.broadcasted_iota(jnp.int32, (tr, tr), 0)
    cc = lax.broadcasted_iota(jnp.int32, (tr, tr), 1)
    tri = (cc < rr).astype(BF16)
    prefix = jnp.dot(tri, onehot.astype(BF16), preferred_element_type=F32) + carry_ref[...]
    k_lane = lax.broadcasted_iota(jnp.int32, (tr, TOP_K), 1)
    gates = jnp.zeros((tr, TOP_K), F32)
    eidx = jnp.zeros((tr, TOP_K), F32)
    rank = jnp.zeros((tr, TOP_K), F32)
    for k in range(TOP_K):
        rk = jnp.sum(jnp.where(hots[k], prefix, 0.0), axis=-1, keepdims=True)
        gates = jnp.where(k_lane == k, exps[k] / den, gates)
        eidx = jnp.where(k_lane == k, idxs[k], eidx)
        rank = jnp.where(k_lane == k, rk, rank)
    gate_ref[...] = gates
    eidx_ref[...] = eidx.astype(jnp.int32)
    rank_ref[...] = rank.astype(jnp.int32)
    carry_ref[...] = carry_ref[...] + jnp.sum(onehot, axis=0, keepdims=True)
    cnt_ref[...] = carry_ref[...].astype(jnp.int32)


def _route(logits, tr):
    T, E = logits.shape
    return pl.pallas_call(
        _route_kernel,
        grid=(T // tr,),
        in_specs=[pl.BlockSpec((tr, E), lambda i: (i, 0))],
        out_specs=[
            pl.BlockSpec((tr, TOP_K), lambda i: (i, 0)),
            pl.BlockSpec((tr, TOP_K), lambda i: (i, 0)),
            pl.BlockSpec((tr, TOP_K), lambda i: (i, 0)),
            pl.BlockSpec((1, E), lambda i: (0, 0)),
        ],
        out_shape=[
            jax.ShapeDtypeStruct((T, TOP_K), F32),
            jax.ShapeDtypeStruct((T, TOP_K), jnp.int32),
            jax.ShapeDtypeStruct((T, TOP_K), jnp.int32),
            jax.ShapeDtypeStruct((1, E), jnp.int32),
        ],
        scratch_shapes=[pltpu.VMEM((1, E), F32)],
        compiler_params=_cparams(("arbitrary",)),
        name="route",
    )(logits)


ROW_DMA_UNROLL = 8


def _dispatch_kernel(zrow_ref, nblk_ref, na_ref, dest_ref, h_ref, xs_ref, zero_ref, sem, zsem, *, tmE, NB):
    i = pl.program_id(0)
    td = h_ref.shape[0]
    E = zrow_ref.shape[0]

    def zero_copy(row):
        return pltpu.make_async_copy(zero_ref, xs_ref.at[pl.ds(row, tmE)], zsem)

    @pl.when(i == 0)
    def _():
        zero_ref[...] = jnp.zeros(zero_ref.shape, zero_ref.dtype)
        for e in range(E):
            @pl.when(nblk_ref[e] > 0)
            def _(e=e):
                zero_copy(pl.multiple_of(zrow_ref[e], tmE)).start()

        def ztail(j, carry):
            zero_copy(pl.multiple_of(j * tmE, tmE)).start()
            return carry

        lax.fori_loop(na_ref[0], NB, ztail, 0)
        for e in range(E):
            @pl.when(nblk_ref[e] > 0)
            def _(e=e):
                zero_copy(0).wait()

        def zwait(j, carry):
            zero_copy(0).wait()
            return carry

        lax.fori_loop(na_ref[0], NB, zwait, 0)

    def row_copy(r, dst):
        return pltpu.make_async_copy(h_ref.at[r], xs_ref.at[dst], sem)

    def issue(r, carry):
        for k in range(TOP_K):
            row_copy(r, dest_ref[0, 0, r * TOP_K + k]).start(priority=k % 2)
        return carry

    lax.fori_loop(0, td, issue, 0, unroll=ROW_DMA_UNROLL)

    def drain(r, carry):
        for k in range(TOP_K):
            row_copy(0, 0).wait()
        return carry

    lax.fori_loop(0, td, drain, 0, unroll=ROW_DMA_UNROLL)


def _dispatch(h2p, dest, zrow, nblk, n_act, P, tmE, td):
    T, R, _ = h2p.shape
    dest3 = dest.reshape(T // td, 1, td * TOP_K)
    grid_spec = pltpu.PrefetchScalarGridSpec(
        num_scalar_prefetch=3,
        grid=(T // td,),
        in_specs=[
            pl.BlockSpec((1, 1, td * TOP_K), lambda i, *_: (i, 0, 0), memory_space=pltpu.SMEM),
            pl.BlockSpec((td, R, LANES), lambda i, *_: (i, 0, 0)),
        ],
        out_specs=pl.BlockSpec(memory_space=pl.ANY),
        scratch_shapes=[pltpu.VMEM((tmE, R, LANES), h2p.dtype), pltpu.SemaphoreType.DMA(()),
                        pltpu.SemaphoreType.DMA(())],
    )
    return pl.pallas_call(
        functools.partial(_dispatch_kernel, tmE=tmE, NB=P // tmE),
        grid_spec=grid_spec,
        out_shape=jax.ShapeDtypeStruct((P, R, LANES), h2p.dtype),
        compiler_params=_cparams(("arbitrary",)),
        name="dispatch",
    )(zrow, nblk, n_act, dest3, h2p)


MOE_SUB_ROWS = 256


def _moe_kernel(be_ref, na_ref, nv_ref, x_ref, wg_ref, bg_ref, wu_ref, bu_ref, wd_ref, bd_ref, y_ref,
                xb_ref, acc_ref, wgb_ref, wub_ref, wdb_ref):
    j = pl.program_id(0)
    f = pl.program_id(1)
    tmE = y_ref.shape[0]

    @pl.when(jnp.logical_and(j >= na_ref[0], f == 0))
    def _():
        y_ref[...] = jnp.zeros(y_ref.shape, y_ref.dtype)

    @pl.when(j < na_ref[0])
    def _():
        @pl.when(f == 0)
        def _():
            xw = _from_row_tiles(x_ref[...])
            half = xw.shape[1]
            lo, hi = _unpack_bf16_pairs(xw)
            xb_ref[:, :half] = lo.astype(BF16)
            xb_ref[:, half:] = hi.astype(BF16)
            acc_ref[...] = jnp.zeros(acc_ref.shape, F32) + bd_ref[...]

        valid = nv_ref[j]

        def expert_mlp(rows, wg, wu, wd):
            xb = xb_ref[rows, :]
            gl = jnp.dot(xb, wg, preferred_element_type=F32) + bg_ref[...]
            up = jnp.dot(xb, wu, preferred_element_type=F32) + bu_ref[...]
            gl = jnp.minimum(gl, SWIGLU_LIMIT)
            up = jnp.clip(up, -SWIGLU_LIMIT, SWIGLU_LIMIT)
            act = (up + 1.0) * gl * _sigmoid(SWIGLU_ALPHA * gl)
            acc_ref[rows, :] += jnp.dot(act.astype(BF16), wd, preferred_element_type=F32)

        @pl.when(valid == tmE)
        def _():
            expert_mlp(slice(0, tmE), wg_ref[...].astype(BF16), wu_ref[...].astype(BF16),
                       wd_ref[...].astype(BF16))

        @pl.when(valid < tmE)
        def _():
            wgb_ref[...] = wg_ref[...].astype(BF16)
            wub_ref[...] = wu_ref[...].astype(BF16)
            wdb_ref[...] = wd_ref[...].astype(BF16)

        for sb in range(tmE // MOE_SUB_ROWS):
            @pl.when(jnp.logical_and(valid < tmE, sb * MOE_SUB_ROWS < valid))
            def _(sb=sb):
                expert_mlp(slice(sb * MOE_SUB_ROWS, (sb + 1) * MOE_SUB_ROWS),
                           wgb_ref[...], wub_ref[...], wdb_ref[...])

        @pl.when(f == pl.num_programs(1) - 1)
        def _():
            y_ref[...] = _to_row_tiles(_pack_bf16_pairs(acc_ref[...].astype(BF16)))


def _moe(blk_e, n_act, n_valid, xs, w_gate, b_gate, w_up, b_up, w_down, b_down, tmE, tf):
    P, R, _ = xs.shape
    E, D, Fd = w_gate.shape
    NB = P // tmE
    NF = Fd // tf

    def jj(j, na):
        return jnp.minimum(j, na[0] - 1)

    def ff(j, f, na):
        return jnp.where(j < na[0], f, NF - 1)

    grid_spec = pltpu.PrefetchScalarGridSpec(
        num_scalar_prefetch=3,
        grid=(NB, NF),
        in_specs=[
            pl.BlockSpec((tmE, R, LANES), lambda j, f, be, na, nv: (jj(j, na), 0, 0)),
            pl.BlockSpec((None, D, tf), lambda j, f, be, na, nv: (be[jj(j, na)], 0, ff(j, f, na))),
            pl.BlockSpec((None, 1, tf), lambda j, f, be, na, nv: (be[jj(j, na)], 0, ff(j, f, na))),
            pl.BlockSpec((None, D, tf), lambda j, f, be, na, nv: (be[jj(j, na)], 0, ff(j, f, na))),
            pl.BlockSpec((None, 1, tf), lambda j, f, be, na, nv: (be[jj(j, na)], 0, ff(j, f, na))),
            pl.BlockSpec((None, tf, D), lambda j, f, be, na, nv: (be[jj(j, na)], ff(j, f, na), 0)),
            pl.BlockSpec((None, 1, D), lambda j, f, be, na, nv: (be[jj(j, na)], 0, 0)),
        ],
        out_specs=pl.BlockSpec((tmE, R, LANES), lambda j, f, be, na, nv: (j, 0, 0)),
        scratch_shapes=[pltpu.VMEM((tmE, D), BF16), pltpu.VMEM((tmE, D), F32), pltpu.VMEM((D, tf), BF16),
                        pltpu.VMEM((D, tf), BF16), pltpu.VMEM((tf, D), BF16)],
    )
    return pl.pallas_call(
        _moe_kernel,
        grid_spec=grid_spec,
        out_shape=jax.ShapeDtypeStruct((P, R, LANES), jnp.uint32),
        compiler_params=_cparams(("arbitrary", "arbitrary")),
        name="moe",
    )(blk_e, n_act, n_valid, xs, w_gate, b_gate.reshape(E, 1, Fd), w_up, b_up.reshape(E, 1, Fd),
      w_down, b_down.reshape(E, 1, D))


def _combine_kernel(dest_ref, dnext_ref, x1_ref, gate_ref, g2_ref, fg_ref, ys_ref, o_ref, buf_ref, sems):
    i = pl.program_id(0)
    nsteps = pl.num_programs(0)
    tc = x1_ref.shape[0]
    slot = i % 2

    def row_copy(s, r, k, src):
        return pltpu.make_async_copy(ys_ref.at[src], buf_ref.at[s, k, r], sems.at[s])

    def gather(s, idx_ref):
        def issue(r, carry):
            for k in range(TOP_K):
                row_copy(s, r, k, idx_ref[0, 0, r * TOP_K + k]).start(priority=k % 2)
            return carry

        lax.fori_loop(0, tc, issue, 0, unroll=ROW_DMA_UNROLL)

    @pl.when(i == 0)
    def _():
        gather(0, dest_ref)

    @pl.when(i + 1 < nsteps)
    def _():
        gather(1 - slot, dnext_ref)

    def drain(r, carry):
        for k in range(TOP_K):
            row_copy(slot, 0, 0, 0).wait()
        return carry

    lax.fori_loop(0, tc, drain, 0, unroll=ROW_DMA_UNROLL)

    gates = gate_ref[...]
    moe = jnp.zeros(x1_ref.shape, F32)
    for k in range(TOP_K):
        lo, hi = _unpack_bf16_pairs(_from_row_tiles(buf_ref[slot, k]))
        yk = jnp.concatenate([lo, hi], axis=1)
        moe = moe + yk * gates[:, k:k + 1]
    x2 = x1_ref[...] + g2_ref[0] * moe
    y = x2 * lax.rsqrt(jnp.mean(x2 * x2, axis=-1, keepdims=True) + EPS)
    o_ref[...] = y * fg_ref[...]


def _combine(dest, x1, gates, mods3, final_g, ys, tc, seq_len):
    T, D = x1.shape
    nsteps = T // tc
    dest3 = dest.reshape(nsteps, 1, tc * TOP_K)
    brow = lambda i: (i * tc) // seq_len
    return pl.pallas_call(
        _combine_kernel,
        grid=(nsteps,),
        in_specs=[
            pl.BlockSpec((1, 1, tc * TOP_K), lambda i: (i, 0, 0), memory_space=pltpu.SMEM),
            pl.BlockSpec((1, 1, tc * TOP_K), lambda i: (jnp.minimum(i + 1, nsteps - 1), 0, 0),
                         memory_space=pltpu.SMEM),
            pl.BlockSpec((tc, D), lambda i: (i, 0)),
            pl.BlockSpec((tc, TOP_K), lambda i: (i, 0)),
            pl.BlockSpec((1, 1, D), lambda i: (brow(i), 0, 5)),
            pl.BlockSpec((1, D), lambda i: (0, 0)),
            pl.BlockSpec(memory_space=pl.ANY),
        ],
        out_specs=pl.BlockSpec((tc, D), lambda i: (i, 0)),
        out_shape=jax.ShapeDtypeStruct((T, D), F32),
        scratch_shapes=[pltpu.VMEM((2, TOP_K, tc) + ys.shape[1:], ys.dtype), pltpu.SemaphoreType.DMA((2,))],
        compiler_params=_cparams(("arbitrary",)),
        name="combine",
    )(dest3, dest3, x1, gates, mods3, final_g.reshape(1, D), ys)


def _pick(n, pref):
    t = min(n, pref)
    while n % t:
        t //= 2
    return t


def _rope_tables(L, hd):
    quarter = hd // 4
    pos = np.arange(L)
    inv = ROPE_THETA ** (-jnp.arange(quarter, dtype=F32) / quarter)
    tabs_c, tabs_s = [], []
    for p in (pos // GRID_W, pos % GRID_W):
        ang = jnp.asarray(p, F32)[:, None] * inv[None, :]
        c, s = jnp.cos(ang), jnp.sin(ang)
        tabs_c += [c, c]
        tabs_s += [-s, s]
    return jnp.concatenate(tabs_c, axis=1), jnp.concatenate(tabs_s, axis=1)


def _layer(x, c, ctx, c_ctx, ada_w, ada_b, norm1_g, norm2_g, w_in, conv_w, conv_b, lru_w_a, lru_b_a,
           lru_w_x, lru_b_x, lru_lambda, ret_norm_g, lru_norm_g, w_out, router_w, router_b,
           w_gate, b_gate, w_up, b_up, w_down, b_down, final_norm_g):
    B, L, D = x.shape
    Lc = ctx.shape[1]
    T = B * L
    Dl = conv_w.shape[-1]
    Dr = (w_in.shape[1] - 2 * Dl) // 4
    hd = Dr // RET_HEADS
    E = router_w.shape[1]
    assert Dr == Dl and hd == 2 * LANES and Dl == LRU_BLOCKS * LRU_BD
    cw = Dr

    R = ((B + 1 + SUBLANES - 1) // SUBLANES) * SUBLANES
    cond = jnp.zeros((R, D), F32).at[:B].set(c).at[B].set(c_ctx)
    mods = _ada(cond, ada_w, ada_b, tn=_pick(6 * D, 1536))
    mods3 = mods.reshape(R, 1, 6 * D)

    w_in_bf = w_in.astype(BF16)
    cos_t, sin_t = _rope_tables(L, hd)
    tm = _pick(L, 1024)
    x2d = x.reshape(T, D)
    proj = _in_proj(x2d, mods3, lambda i: (i * tm) // L, norm1_g, w_in_bf, cos_t, sin_t,
                    ("q_rope", "k_rope", "plain", "plain", "plain", "plain"), tm, L, hd)
    tmc = _pick(Lc, 512)
    w_ctx = jnp.concatenate([w_in_bf[:, cw:2 * cw], w_in_bf[:, 2 * cw:3 * cw], w_in_bf[:, 4 * cw:5 * cw]], axis=1)
    proj_c = _in_proj(ctx.reshape(B * Lc, D), mods3, lambda i: B, norm1_g, w_ctx, cos_t, sin_t,
                      ("k_plain", "plain", "plain"), tmc, Lc, hd)

    log_g = jnp.log1p(-(2.0 ** (-5.0 - jnp.arange(RET_HEADS, dtype=F32))))
    ret = _retention(log_g, proj, (0, 1, 2, 3), proj_c, (0, 1), ret_norm_g, B, L, Lc, hd)

    w_cat = jnp.concatenate([lru_w_a, lru_w_x], axis=-1).astype(BF16)
    tL = _pick(L, 32)
    tLc = _pick(Lc, 32)
    h_zero = jnp.zeros((B, Dl), F32)
    lru_args = lambda d: (conv_w, conv_b, w_cat[d], lru_b_a[d], lru_b_x[d], lru_lambda[d])
    proj3 = proj.reshape(B, L, proj.shape[1])
    proj_c3 = proj_c.reshape(B, Lc, proj_c.shape[1])
    _, h_f = _lru(proj_c3, 2, h_zero, *lru_args(0), reverse=False, tL=tLc)
    _, h_b = _lru(proj_c3, 2, h_zero, *lru_args(1), reverse=True, tL=tLc)
    hf3, _ = _lru(proj3, 4, h_f, *lru_args(0), reverse=False, tL=tL)
    lru, _ = _lru(proj3, 4, h_b, *lru_args(1), reverse=True, tL=tL, merge_args=(hf3, 5, lru_norm_g))
    lru = lru.reshape(T, Dl)

    x1, h2, logits = _out_proj(ret, lru, x2d, mods3, norm2_g, w_out.astype(BF16),
                               router_w.astype(BF16), router_b, _pick(L, 256), L)

    gates, eidx, rank, counts = _route(logits, _pick(T, 1024))
    tmE = _pick(T, 1024)
    counts = counts.reshape(E)
    nblk = (counts + tmE - 1) // tmE
    blk_end = jnp.cumsum(nblk)
    pad_start = (blk_end - nblk) * tmE
    NB = (T * TOP_K) // tmE + E
    n_act = blk_end[-1:].astype(jnp.int32)
    blk_e = jnp.minimum(jnp.sum(blk_end[None, :] <= jnp.arange(NB)[:, None], axis=1), E - 1).astype(jnp.int32)
    zrow = (jnp.maximum(blk_end - 1, 0) * tmE).astype(jnp.int32)
    expert_hot = eidx[..., None] == jnp.arange(E, dtype=jnp.int32)
    dest = (jnp.sum(jnp.where(expert_hot, pad_start.astype(jnp.int32), 0), axis=-1) + rank).astype(jnp.int32)

    blk_in_e = jnp.arange(NB) - (blk_end - nblk)[blk_e]
    n_valid = jnp.where(jnp.arange(NB) < n_act[0],
                        jnp.clip(counts[blk_e] - blk_in_e * tmE, 0, tmE), 0).astype(jnp.int32)

    P = NB * tmE
    xs = _dispatch(h2, dest, zrow, nblk.astype(jnp.int32), n_act, P, tmE, _pick(T, 256))
    ys = _moe(blk_e, n_act, n_valid, xs, w_gate, b_gate, w_up, b_up, w_down, b_down, tmE,
              _pick(w_gate.shape[-1], 256))
    out = _combine(dest, x1, gates, mods3, final_norm_g, ys, _pick(T, 256), L)
    return out.reshape(B, L, D)


def kernel(x, c, ctx, c_ctx, ada_w, ada_b, norm1_g, norm2_g, w_in, conv_w, conv_b, lru_w_a, lru_b_a,
           lru_w_x, lru_b_x, lru_lambda, ret_norm_g, lru_norm_g, w_out, router_w, router_b,
           w_gate, b_gate, w_up, b_up, w_down, b_down, final_norm_g):
    assert ada_w.shape[0] == 1, "single-layer configuration"
    return _layer(x, c, ctx, c_ctx, ada_w[0], ada_b[0], norm1_g[0], norm2_g[0], w_in[0], conv_w[0], conv_b[0],
                  lru_w_a[0], lru_b_a[0], lru_w_x[0], lru_b_x[0], lru_lambda[0], ret_norm_g[0], lru_norm_g[0],
                  w_out[0], router_w[0], router_b[0], w_gate[0], b_gate[0], w_up[0], b_up[0], w_down[0],
                  b_down[0], final_norm_g)
```

```python
import functools

import jax
import jax.numpy as jnp
import numpy as np
from jax import lax
from jax.experimental import pallas as pl
from jax.experimental.pallas import tpu as pltpu

RET_HEADS = 4
LRU_BLOCKS = 8
LRU_BD = 128
CONV_W = 4
GRID_W = 64
TOP_K = 4
LRU_C = 8.0
SWIGLU_LIMIT = 7.0
SWIGLU_ALPHA = 1.702
ROPE_THETA = 10000.0
EPS = 1e-6

LANES = 128
SUBLANES = 8
VMEM_LIMIT = 56 * 1024 * 1024

F32 = jnp.float32
BF16 = jnp.bfloat16


def _cparams(sem):
    return pltpu.CompilerParams(dimension_semantics=sem, vmem_limit_bytes=VMEM_LIMIT)


def _sigmoid(z):
    return 0.5 * jnp.tanh(0.5 * z) + 0.5


def _pack_bf16_pairs(v):
    n = v.shape[1] // 2
    lo = lax.bitcast_convert_type(v[:, :n].astype(F32), jnp.uint32)
    hi = lax.bitcast_convert_type(v[:, n:].astype(F32), jnp.uint32)
    return (lo >> 16) | (hi & jnp.uint32(0xFFFF0000))


def _to_row_tiles(v):
    n = v.shape[1] // LANES
    st = jnp.stack([v[:, j * LANES:(j + 1) * LANES] for j in range(n)], axis=0)
    return pltpu.einshape("jtl->tjl", st)


def _from_row_tiles(v3):
    st = pltpu.einshape("tjl->jtl", v3)
    return jnp.concatenate([st[j] for j in range(st.shape[0])], axis=1)


def _unpack_bf16_pairs(w):
    lo = lax.bitcast_convert_type(w << 16, F32)
    hi = lax.bitcast_convert_type(w & jnp.uint32(0xFFFF0000), F32)
    return lo, hi


def _ada_kernel(c_ref, w_ref, b_ref, o_ref):
    c = c_ref[...]
    s = (c * _sigmoid(c)).astype(BF16)
    o_ref[...] = jnp.dot(s, w_ref[...].astype(BF16), preferred_element_type=F32) + b_ref[...]


def _ada(cond, w, b, tn):
    R, D = cond.shape
    N = w.shape[1]
    return pl.pallas_call(
        _ada_kernel,
        grid=(N // tn,),
        in_specs=[
            pl.BlockSpec((R, D), lambda j: (0, 0)),
            pl.BlockSpec((D, tn), lambda j: (0, j)),
            pl.BlockSpec((1, tn), lambda j: (0, j)),
        ],
        out_specs=pl.BlockSpec((R, tn), lambda j: (0, j)),
        out_shape=jax.ShapeDtypeStruct((R, N), F32),
        compiler_params=_cparams(("arbitrary",)),
        name="ada",
    )(cond, w, b.reshape(1, N))


def _in_proj_kernel(x_ref, sc_ref, sh_ref, g_ref, w_ref, cos_ref, sin_ref, o, xn_ref, *, kinds, hd, scale):
    n = pl.program_id(1)

    @pl.when(n == 0)
    def _():
        x = x_ref[...]
        y = x * lax.rsqrt(jnp.mean(x * x, axis=-1, keepdims=True) + EPS)
        y = y * g_ref[...]
        y = y * (1.0 + sc_ref[0]) + sh_ref[0]
        xn_ref[...] = y.astype(BF16)

    def matmul():
        return jnp.dot(xn_ref[...], w_ref[...], preferred_element_type=F32)

    def rope(a):
        pieces = []
        n_grp = a.shape[1] // LANES
        per_head = hd // LANES
        for gi in range(n_grp):
            sl = a[:, gi * LANES:(gi + 1) * LANES]
            t = gi % per_head
            c = cos_ref[:, t * LANES:(t + 1) * LANES]
            s = sin_ref[:, t * LANES:(t + 1) * LANES]
            pieces.append(sl * c + pltpu.roll(sl, LANES // 2, 1) * s)
        return jnp.concatenate(pieces, axis=1)

    for kind in dict.fromkeys(kinds):
        cond = functools.reduce(jnp.logical_or, [n == idx for idx, kd in enumerate(kinds) if kd == kind])

        @pl.when(cond)
        def _(kind=kind):
            if kind == "q_rope":
                o[...] = rope(matmul()).astype(o.dtype)
            elif kind == "k_rope":
                o[...] = (rope(matmul()) * scale).astype(o.dtype)
            elif kind == "k_plain":
                o[...] = (matmul() * scale).astype(o.dtype)
            else:
                o[...] = matmul().astype(o.dtype)


def _in_proj(x2d, mods3, mod_row_fn, norm_g, w_bf, cos_t, sin_t, kinds, tm, seq_len, hd):
    T, D = x2d.shape
    n_out = len(kinds)
    cw = w_bf.shape[1] // n_out
    tiles_per_seq = seq_len // tm
    in_specs = [
        pl.BlockSpec((tm, D), lambda i, n: (i, 0)),
        pl.BlockSpec((1, 1, D), lambda i, n: (mod_row_fn(i), 0, 1)),
        pl.BlockSpec((1, 1, D), lambda i, n: (mod_row_fn(i), 0, 0)),
        pl.BlockSpec((1, D), lambda i, n: (0, 0)),
        pl.BlockSpec((D, cw), lambda i, n: (0, n)),
        pl.BlockSpec((tm, hd), lambda i, n: (i % tiles_per_seq, 0)),
        pl.BlockSpec((tm, hd), lambda i, n: (i % tiles_per_seq, 0)),
    ]
    return pl.pallas_call(
        functools.partial(_in_proj_kernel, kinds=tuple(kinds), hd=hd, scale=float(hd) ** -0.5),
        grid=(T // tm, n_out),
        in_specs=in_specs,
        out_specs=pl.BlockSpec((tm, cw), lambda i, n: (i, n)),
        out_shape=jax.ShapeDtypeStruct((T, n_out * cw), BF16),
        scratch_shapes=[pltpu.VMEM((tm, D), BF16)],
        compiler_params=_cparams(("arbitrary", "arbitrary")),
        name="in_proj",
    )(x2d, mods3, mods3, norm_g.reshape(1, D), w_bf, cos_t, sin_t)


def _dot_t0(a, b):
    return lax.dot_general(a, b, (((0,), (0,)), ((), ())), preferred_element_type=F32)


def _dot_t1(a, b):
    return lax.dot_general(a, b, (((1,), (1,)), ((), ())), preferred_element_type=F32)


RET_CHUNK = 256
RET_UNROLL = 4


def _ret_kernel(lg_ref, q_ref, k_ref, v_ref, g_ref, kc_ref, vc_ref, gn_ref, o_ref,
                oacc_ref, s_ref, sf_ref, dec_ref, *, C, n_chunks, n_cchunks):
    hd = q_ref.shape[1]
    lg = lg_ref[pl.program_id(0)]

    @pl.when(pl.program_id(1) == 0)
    def _():
        ri = lax.broadcasted_iota(jnp.int32, (C, C), 0).astype(F32)
        ci = lax.broadcasted_iota(jnp.int32, (C, C), 1).astype(F32)
        pos = lax.broadcasted_iota(jnp.int32, (C, hd), 0).astype(F32)
        dec_ref[0] = jnp.exp(jnp.abs(ri - ci) * lg)
        dec_ref[1] = jnp.exp((pos + 1.0) * lg)
        dec_ref[2] = jnp.exp((float(C) - pos) * lg)
        dec_ref[3] = jnp.exp((float(C) - 1.0 - pos) * lg)
        dec_ref[4] = jnp.exp(pos * lg)

    DSYM, QD_F, QD_B, KD_F, KD_B = range(5)
    cd = jnp.exp(jnp.full((1, 1), float(C), F32) * lg)

    def kv_update(s, kk, vv, kd_slot):
        kw = (kk.astype(F32) * dec_ref[kd_slot]).astype(BF16)
        upd = _dot_t0(kw, vv)
        return upd if s is None else s * cd + upd

    s_f = None
    for c in range(n_cchunks):
        s_f = kv_update(s_f, kc_ref[c * C:(c + 1) * C, :], vc_ref[c * C:(c + 1) * C, :], KD_F)
    sf_ref[...] = s_f
    s_b = None
    for c in reversed(range(n_cchunks)):
        s_b = kv_update(s_b, kc_ref[c * C:(c + 1) * C, :], vc_ref[c * C:(c + 1) * C, :], KD_B)

    s_ref[...] = s_b

    def bwd(idx, carry):
        n = n_chunks - 1 - idx
        rows = pl.ds(pl.multiple_of(n * C, C), C)
        s = s_ref[...]
        oacc_ref[rows, :] = jnp.dot(q_ref[rows, :], s.astype(BF16), preferred_element_type=F32) * dec_ref[QD_B]
        s_ref[...] = kv_update(s, k_ref[rows, :], v_ref[rows, :], KD_B)
        return carry

    lax.fori_loop(0, n_chunks, bwd, 0, unroll=RET_UNROLL)

    s_ref[...] = sf_ref[...]

    def fwd(n, carry):
        rows = pl.ds(pl.multiple_of(n * C, C), C)
        s = s_ref[...]
        qn = q_ref[rows, :]
        kn = k_ref[rows, :]
        vn = v_ref[rows, :]
        oacc_ref[rows, :] += jnp.dot(qn, s.astype(BF16), preferred_element_type=F32) * dec_ref[QD_F]
        scores = (_dot_t1(qn, kn) * dec_ref[DSYM]).astype(BF16)
        o = oacc_ref[rows, :] + jnp.dot(scores, vn, preferred_element_type=F32)
        s_ref[...] = kv_update(s, kn, vn, KD_F)
        mu = jnp.mean(o, axis=-1, keepdims=True)
        d = o - mu
        var = jnp.mean(d * d, axis=-1, keepdims=True)
        y = d * lax.rsqrt(var + EPS)
        gate = g_ref[rows, :].astype(F32)
        y = y * gn_ref[...] * (gate * _sigmoid(gate))
        o_ref[rows, :] = y.astype(o_ref.dtype)
        return carry

    lax.fori_loop(0, n_chunks, fwd, 0, unroll=RET_UNROLL)


def _retention(log_g, proj, cols, proj_c, cols_c, gn, B, L, Lc, hd):
    T = proj.shape[0]
    H = gn.shape[0] // hd
    C = RET_CHUNK
    assert C == hd
    col = lambda chunk: (lambda h, b, *_: (b, chunk * H + h))
    grid_spec = pltpu.PrefetchScalarGridSpec(
        num_scalar_prefetch=1,
        grid=(H, B),
        in_specs=[pl.BlockSpec((L, hd), col(c)) for c in cols]
        + [pl.BlockSpec((Lc, hd), col(c)) for c in cols_c]
        + [pl.BlockSpec((1, hd), lambda h, b, *_: (0, h))],
        out_specs=pl.BlockSpec((L, hd), lambda h, b, *_: (b, h)),
        scratch_shapes=[pltpu.VMEM((L, hd), F32), pltpu.VMEM((hd, hd), F32), pltpu.VMEM((hd, hd), F32),
                        pltpu.VMEM((5, C, hd), F32)],
    )
    return pl.pallas_call(
        functools.partial(_ret_kernel, C=C, n_chunks=L // C, n_cchunks=Lc // C),
        grid_spec=grid_spec,
        out_shape=jax.ShapeDtypeStruct((T, H * hd), BF16),
        compiler_params=_cparams(("arbitrary", "arbitrary")),
        name="retention",
    )(log_g, proj, proj, proj, proj, proj_c, proj_c, gn.reshape(1, H * hd))


def _gelu_tanh(y):
    k = 0.7978845608028654
    return (0.5 * y) * (1.0 + jnp.tanh(y * (k + (k * 0.044715) * (y * y))))


def _lru_kernel(*refs, reverse, merge, B, tL, nT):
    if merge:
        (u_ref, up_ref, un_ref, h0_ref, cw_ref, cb_ref, w_ref, ba_ref, bx_ref, lam_ref,
         hf_ref, y_ref, gn_ref, out_ref, hlast_ref, a_ref, b_ref, ho_ref, hc_ref) = refs
    else:
        (u_ref, up_ref, un_ref, h0_ref, cw_ref, cb_ref, w_ref, ba_ref, bx_ref, lam_ref,
         out_ref, hlast_ref, a_ref, b_ref, ho_ref, hc_ref) = refs
    j = pl.program_id(0)
    tile = (nT - 1 - j) if reverse else j
    G = LRU_BLOCKS
    bd = LRU_BD
    Dl = G * bd
    halo = up_ref.shape[1]

    @pl.when(j == 0)
    def _():
        for g in range(G):
            hc_ref[g] = h0_ref[:, g * bd:(g + 1) * bd]

    cur = pltpu.einshape("btd->tbd", u_ref[...].astype(F32)).reshape(tL * B, Dl)
    prev = jnp.where(tile > 0, up_ref[:, halo - 1, :].astype(F32), 0.0)
    nxt = [jnp.where(tile < nT - 1, un_ref[:, t, :].astype(F32), 0.0) for t in range(CONV_W - 2)]
    ext = jnp.concatenate([prev, cur] + nxt, axis=0)

    uc = jnp.zeros((tL * B, Dl), F32) + cb_ref[...]
    for tap in range(CONV_W):
        uc = uc + ext[tap * B:(tap + tL) * B, :] * cw_ref[tap:tap + 1, :]

    lam = lam_ref[...]
    nl = -lam
    softplus = jnp.maximum(nl, 0.0) + jnp.log(1.0 + jnp.exp(-jnp.abs(nl)))
    cfac = -LRU_C * softplus

    for g in range(G):
        sl = slice(g * bd, (g + 1) * bd)
        xg = uc[:, sl]
        half = jnp.dot(xg.astype(BF16), w_ref[g], preferred_element_type=F32)
        tr = jnp.tanh(half[:, :bd] + 0.5 * ba_ref[:, sl])
        ti = jnp.tanh(half[:, bd:] + 0.5 * bx_ref[:, sl])
        hc = 0.5 * cfac[:, sl]
        a = jnp.exp(hc * tr + hc)
        a_ref[g] = a
        b_ref[g] = jnp.sqrt(1.0 - a * a) * ((ti + 1.0) * (0.5 * xg))

    def step(s, hs):
        t = (tL - 1 - s) if reverse else s
        rows = pl.ds(pl.multiple_of(t * B, B), B)
        new = []
        for g in range(G):
            hv = a_ref[g, rows, :] * hs[g] + b_ref[g, rows, :]
            ho_ref[g, rows, :] = hv
            new.append(hv)
        return tuple(new)

    hs = lax.fori_loop(0, tL, step, tuple(hc_ref[g] for g in range(G)), unroll=4)
    for g in range(G):
        hc_ref[g] = hs[g]
        hlast_ref[:, g * bd:(g + 1) * bd] = hs[g]

    def batch_major(g):
        return pltpu.einshape("tbd->btd", ho_ref[g].reshape(tL, B, bd))

    if not merge:
        for g in range(G):
            out_ref[:, :, g * bd:(g + 1) * bd] = batch_major(g).astype(out_ref.dtype)
    else:
        ss = jnp.zeros((B, tL, 1), F32)
        for g in range(G):
            sl = slice(g * bd, (g + 1) * bd)
            hsum = hf_ref[:, :, sl].astype(F32) + batch_major(g)
            z = hsum * _gelu_tanh(y_ref[:, :, sl].astype(F32))
            ho_ref[g] = z.reshape(B * tL, bd)
            ss = ss + jnp.sum(z * z, axis=-1, keepdims=True)
        inv = lax.rsqrt(ss * (1.0 / (G * bd)) + EPS)
        for g in range(G):
            sl = slice(g * bd, (g + 1) * bd)
            z = ho_ref[g].reshape(B, tL, bd)
            out_ref[:, :, sl] = (z * inv * gn_ref[:, sl]).astype(out_ref.dtype)


LRU_HALO = 16


def _lru(proj3, ucol, h0, conv_w, conv_b, w_cat, b_a, b_x, lam, *, reverse, tL, merge_args=None):
    B, L, _ = proj3.shape
    Dl = conv_w.shape[-1]
    nT = L // tL
    merge = merge_args is not None
    tile = (lambda j: nT - 1 - j) if reverse else (lambda j: j)
    hb = tL // LRU_HALO
    n_halo = L // LRU_HALO
    row = lambda a: a.reshape(1, Dl)
    in_specs = [
        pl.BlockSpec((B, tL, Dl), lambda j: (0, tile(j), ucol)),
        pl.BlockSpec((B, LRU_HALO, Dl), lambda j: (0, jnp.maximum(tile(j) * hb - 1, 0), ucol)),
        pl.BlockSpec((B, LRU_HALO, Dl), lambda j: (0, jnp.minimum((tile(j) + 1) * hb, n_halo - 1), ucol)),
        pl.BlockSpec((B, Dl), lambda j: (0, 0)),
        pl.BlockSpec((CONV_W, Dl), lambda j: (0, 0)),
        pl.BlockSpec((1, Dl), lambda j: (0, 0)),
        pl.BlockSpec((LRU_BLOCKS, LRU_BD, 2 * LRU_BD), lambda j: (0, 0, 0)),
        pl.BlockSpec((1, Dl), lambda j: (0, 0)),
        pl.BlockSpec((1, Dl), lambda j: (0, 0)),
        pl.BlockSpec((1, Dl), lambda j: (0, 0)),
    ]
    args = [proj3, proj3, proj3, h0, conv_w, row(conv_b), w_cat, row(b_a), row(b_x), row(lam)]
    if merge:
        hf3, ycol, gn = merge_args
        in_specs += [
            pl.BlockSpec((B, tL, Dl), lambda j: (0, tile(j), 0)),
            pl.BlockSpec((B, tL, Dl), lambda j: (0, tile(j), ycol)),
            pl.BlockSpec((1, Dl), lambda j: (0, 0)),
        ]
        args += [hf3, proj3, row(gn)]
        out_dtype = BF16
    else:
        out_dtype = F32
    out, hlast = pl.pallas_call(
        functools.partial(_lru_kernel, reverse=reverse, merge=merge, B=B, tL=tL, nT=nT),
        grid=(nT,),
        in_specs=in_specs,
        out_specs=[
            pl.BlockSpec((B, tL, Dl), lambda j: (0, tile(j), 0)),
            pl.BlockSpec((B, Dl), lambda j: (0, 0)),
        ],
        out_shape=[jax.ShapeDtypeStruct((B, L, Dl), out_dtype), jax.ShapeDtypeStruct((B, Dl), F32)],
        scratch_shapes=[
            pltpu.VMEM((LRU_BLOCKS, B * tL, LRU_BD), F32),
            pltpu.VMEM((LRU_BLOCKS, B * tL, LRU_BD), F32),
            pltpu.VMEM((LRU_BLOCKS, B * tL, LRU_BD), F32),
            pltpu.VMEM((LRU_BLOCKS, B, LRU_BD), F32),
        ],
        compiler_params=_cparams(("arbitrary",)),
        name="lru_bwd_merge" if merge else ("lru_rev" if reverse else "lru_fwd"),
    )(*args)
    return out, hlast


def _out_proj_kernel(ret_ref, lru_ref, x_ref, g1_ref, sc_ref, sh_ref, ng_ref, wo_ref, rw_ref, rb_ref,
                     x1_ref, h2_ref, lg_ref):
    dr = ret_ref.shape[1]
    mix = jnp.dot(ret_ref[...], wo_ref[0:dr, :], preferred_element_type=F32)
    mix = mix + jnp.dot(lru_ref[...], wo_ref[dr:, :], preferred_element_type=F32)
    x1 = x_ref[...] + g1_ref[0] * mix
    x1_ref[...] = x1
    y = x1 * lax.rsqrt(jnp.mean(x1 * x1, axis=-1, keepdims=True) + EPS)
    y = y * ng_ref[...]
    h2 = (y * (1.0 + sc_ref[0]) + sh_ref[0]).astype(BF16)
    h2_ref[...] = _to_row_tiles(_pack_bf16_pairs(h2))
    lg_ref[...] = jnp.dot(h2, rw_ref[...], preferred_element_type=F32) + rb_ref[...]


def _out_proj(ret, lru, x2d, mods3, norm_g, wo_bf, rw_bf, rb, tm, seq_len):
    T, D = x2d.shape
    E = rw_bf.shape[1]
    Dm = wo_bf.shape[0]
    brow = lambda i: (i * tm) // seq_len
    return pl.pallas_call(
        _out_proj_kernel,
        grid=(T // tm,),
        in_specs=[
            pl.BlockSpec((tm, ret.shape[1]), lambda i: (i, 0)),
            pl.BlockSpec((tm, lru.shape[1]), lambda i: (i, 0)),
            pl.BlockSpec((tm, D), lambda i: (i, 0)),
            pl.BlockSpec((1, 1, D), lambda i: (brow(i), 0, 2)),
            pl.BlockSpec((1, 1, D), lambda i: (brow(i), 0, 4)),
            pl.BlockSpec((1, 1, D), lambda i: (brow(i), 0, 3)),
            pl.BlockSpec((1, D), lambda i: (0, 0)),
            pl.BlockSpec((Dm, D), lambda i: (0, 0)),
            pl.BlockSpec((D, E), lambda i: (0, 0)),
            pl.BlockSpec((1, E), lambda i: (0, 0)),
        ],
        out_specs=[
            pl.BlockSpec((tm, D), lambda i: (i, 0)),
            pl.BlockSpec((tm, D // 2 // LANES, LANES), lambda i: (i, 0, 0)),
            pl.BlockSpec((tm, E), lambda i: (i, 0)),
        ],
        out_shape=[
            jax.ShapeDtypeStruct((T, D), F32),
            jax.ShapeDtypeStruct((T, D // 2 // LANES, LANES), jnp.uint32),
            jax.ShapeDtypeStruct((T, E), F32),
        ],
        compiler_params=_cparams(("arbitrary",)),
        name="out_proj",
    )(ret, lru, x2d, mods3, mods3, mods3, norm_g.reshape(1, D), wo_bf, rw_bf, rb.reshape(1, E))


def _route_kernel(lg_ref, gate_ref, eidx_ref, rank_ref, cnt_ref, carry_ref):
    i = pl.program_id(0)
    tr, E = lg_ref.shape

    @pl.when(i == 0)
    def _():
        carry_ref[...] = jnp.zeros_like(carry_ref)

    work = lg_ref[...]
    lane = lax.broadcasted_iota(jnp.int32, (tr, E), 1).astype(F32)
    vals, idxs, hots = [], [], []
    for _ in range(TOP_K):
        m = jnp.max(work, axis=-1, keepdims=True)
        idx = jnp.min(jnp.where(work == m, lane, float(E)), axis=-1, keepdims=True)
        hot = lane == idx
        vals.append(m)
        idxs.append(idx)
        hots.append(hot)
        work = jnp.where(hot, -jnp.inf, work)
    exps = [jnp.exp(v - vals[0]) for v in vals]
    den = exps[0]
    for e in exps[1:]:
        den = den + e
    onehot = jnp.zeros((tr, E), F32)
    for hot in hots:
        onehot = onehot + hot.astype(F32)
    rr = lax.broadcasted_iota(jnp.int32, (tr, tr), 0)
    cc = lax.broadcasted_iota(jnp.int32, (tr, tr), 1)
    tri = (cc < rr).astype(BF16)
    prefix = jnp.dot(tri, onehot.astype(BF16), preferred_element_type=F32) + carry_ref[...]
    k_lane = lax.broadcasted_iota(jnp.int32, (tr, TOP_K), 1)
    gates = jnp.zeros((tr, TOP_K), F32)
    eidx = jnp.zeros((tr, TOP_K), F32)
    rank = jnp.zeros((tr, TOP_K), F32)
    for k in range(TOP_K):
        rk = jnp.sum(jnp.where(hots[k], prefix, 0.0), axis=-1, keepdims=True)
        gates = jnp.where(k_lane == k, exps[k] / den, gates)
        eidx = jnp.where(k_lane == k, idxs[k], eidx)
        rank = jnp.where(k_lane == k, rk, rank)
    gate_ref[...] = gates
    eidx_ref[...] = eidx.astype(jnp.int32)
    rank_ref[...] = rank.astype(jnp.int32)
    carry_ref[...] = carry_ref[...] + jnp.sum(onehot, axis=0, keepdims=True)
    cnt_ref[...] = carry_ref[...].astype(jnp.int32)


def _route(logits, tr):
    T, E = logits.shape
    return pl.pallas_call(
        _route_kernel,
        grid=(T // tr,),
        in_specs=[pl.BlockSpec((tr, E), lambda i: (i, 0))],
        out_specs=[
            pl.BlockSpec((tr, TOP_K), lambda i: (i, 0)),
            pl.BlockSpec((tr, TOP_K), lambda i: (i, 0)),
            pl.BlockSpec((tr, TOP_K), lambda i: (i, 0)),
            pl.BlockSpec((1, E), lambda i: (0, 0)),
        ],
        out_shape=[
            jax.ShapeDtypeStruct((T, TOP_K), F32),
            jax.ShapeDtypeStruct((T, TOP_K), jnp.int32),
            jax.ShapeDtypeStruct((T, TOP_K), jnp.int32),
            jax.ShapeDtypeStruct((1, E), jnp.int32),
        ],
        scratch_shapes=[pltpu.VMEM((1, E), F32)],
        compiler_params=_cparams(("arbitrary",)),
        name="route",
    )(logits)


ROW_DMA_UNROLL = 8


def _dispatch_kernel(zrow_ref, nblk_ref, na_ref, dest_ref, h_ref, xs_ref, zero_ref, sem, zsem, *, tmE, NB):
    i = pl.program_id(0)
    td = h_ref.shape[0]
    E = zrow_ref.shape[0]

    def zero_copy(row):
        return pltpu.make_async_copy(zero_ref, xs_ref.at[pl.ds(row, tmE)], zsem)

    @pl.when(i == 0)
    def _():
        zero_ref[...] = jnp.zeros(zero_ref.shape, zero_ref.dtype)
        for e in range(E):
            @pl.when(nblk_ref[e] > 0)
            def _(e=e):
                zero_copy(pl.multiple_of(zrow_ref[e], tmE)).start()

        def ztail(j, carry):
            zero_copy(pl.multiple_of(j * tmE, tmE)).start()
            return carry

        lax.fori_loop(na_ref[0], NB, ztail, 0)
        for e in range(E):
            @pl.when(nblk_ref[e] > 0)
            def _(e=e):
                zero_copy(0).wait()

        def zwait(j, carry):
            zero_copy(0).wait()
            return carry

        lax.fori_loop(na_ref[0], NB, zwait, 0)

    def row_copy(r, dst):
        return pltpu.make_async_copy(h_ref.at[r], xs_ref.at[dst], sem)

    def issue(r, carry):
        for k in range(TOP_K):
            row_copy(r, dest_ref[0, 0, r * TOP_K + k]).start(priority=k % 2)
        return carry

    lax.fori_loop(0, td, issue, 0, unroll=ROW_DMA_UNROLL)

    def drain(r, carry):
        for k in range(TOP_K):
            row_copy(0, 0).wait()
        return carry

    lax.fori_loop(0, td, drain, 0, unroll=ROW_DMA_UNROLL)


def _dispatch(h2p, dest, zrow, nblk, n_act, P, tmE, td):
    T, R, _ = h2p.shape
    dest3 = dest.reshape(T // td, 1, td * TOP_K)
    grid_spec = pltpu.PrefetchScalarGridSpec(
        num_scalar_prefetch=3,
        grid=(T // td,),
        in_specs=[
            pl.BlockSpec((1, 1, td * TOP_K), lambda i, *_: (i, 0, 0), memory_space=pltpu.SMEM),
            pl.BlockSpec((td, R, LANES), lambda i, *_: (i, 0, 0)),
        ],
        out_specs=pl.BlockSpec(memory_space=pl.ANY),
        scratch_shapes=[pltpu.VMEM((tmE, R, LANES), h2p.dtype), pltpu.SemaphoreType.DMA(()),
                        pltpu.SemaphoreType.DMA(())],
    )
    return pl.pallas_call(
        functools.partial(_dispatch_kernel, tmE=tmE, NB=P // tmE),
        grid_spec=grid_spec,
        out_shape=jax.ShapeDtypeStruct((P, R, LANES), h2p.dtype),
        compiler_params=_cparams(("arbitrary",)),
        name="dispatch",
    )(zrow, nblk, n_act, dest3, h2p)


MOE_SUB_ROWS = 256


def _moe_kernel(be_ref, na_ref, nv_ref, x_ref, wg_ref, bg_ref, wu_ref, bu_ref, wd_ref, bd_ref, y_ref,
                xb_ref, acc_ref):
    j = pl.program_id(0)
    f = pl.program_id(1)
    tmE = y_ref.shape[0]

    @pl.when(jnp.logical_and(j >= na_ref[0], f == 0))
    def _():
        y_ref[...] = jnp.zeros(y_ref.shape, y_ref.dtype)

    @pl.when(j < na_ref[0])
    def _():
        @pl.when(f == 0)
        def _():
            xw = _from_row_tiles(x_ref[...])
            half = xw.shape[1]
            lo, hi = _unpack_bf16_pairs(xw)
            xb_ref[:, :half] = lo.astype(BF16)
            xb_ref[:, half:] = hi.astype(BF16)
            acc_ref[...] = jnp.zeros(acc_ref.shape, F32) + bd_ref[...]

        valid = nv_ref[j]

        def expert_mlp(rows, wg, wu, wd):
            xb = xb_ref[rows, :]
            gl = jnp.dot(xb, wg, preferred_element_type=F32) + bg_ref[...]
            up = jnp.dot(xb, wu, preferred_element_type=F32) + bu_ref[...]
            gl = jnp.minimum(gl, SWIGLU_LIMIT)
            up = jnp.clip(up, -SWIGLU_LIMIT, SWIGLU_LIMIT)
            act = (up + 1.0) * gl * _sigmoid(SWIGLU_ALPHA * gl)
            acc_ref[rows, :] += jnp.dot(act.astype(BF16), wd, preferred_element_type=F32)

        @pl.when(valid == tmE)
        def _():
            expert_mlp(slice(0, tmE), wg_ref[...].astype(BF16), wu_ref[...].astype(BF16),
                       wd_ref[...].astype(BF16))

        for sb in range(tmE // MOE_SUB_ROWS):
            @pl.when(jnp.logical_and(valid < tmE, sb * MOE_SUB_ROWS < valid))
            def _(sb=sb):
                expert_mlp(slice(sb * MOE_SUB_ROWS, (sb + 1) * MOE_SUB_ROWS), wg_ref[...].astype(BF16),
                           wu_ref[...].astype(BF16), wd_ref[...].astype(BF16))

        @pl.when(f == pl.num_programs(1) - 1)
        def _():
            y_ref[...] = _to_row_tiles(_pack_bf16_pairs(acc_ref[...].astype(BF16)))


def _moe(blk_e, n_act, n_valid, xs, w_gate, b_gate, w_up, b_up, w_down, b_down, tmE, tf):
    P, R, _ = xs.shape
    E, D, Fd = w_gate.shape
    NB = P // tmE
    NF = Fd // tf

    def jj(j, na):
        return jnp.minimum(j, na[0] - 1)

    def ff(j, f, na):
        return jnp.where(j < na[0], f, NF - 1)

    grid_spec = pltpu.PrefetchScalarGridSpec(
        num_scalar_prefetch=3,
        grid=(NB, NF),
        in_specs=[
            pl.BlockSpec((tmE, R, LANES), lambda j, f, be, na, nv: (jj(j, na), 0, 0)),
            pl.BlockSpec((None, D, tf), lambda j, f, be, na, nv: (be[jj(j, na)], 0, ff(j, f, na))),
            pl.BlockSpec((None, 1, tf), lambda j, f, be, na, nv: (be[jj(j, na)], 0, ff(j, f, na))),
            pl.BlockSpec((None, D, tf), lambda j, f, be, na, nv: (be[jj(j, na)], 0, ff(j, f, na))),
            pl.BlockSpec((None, 1, tf), lambda j, f, be, na, nv: (be[jj(j, na)], 0, ff(j, f, na))),
            pl.BlockSpec((None, tf, D), lambda j, f, be, na, nv: (be[jj(j, na)], ff(j, f, na), 0)),
            pl.BlockSpec((None, 1, D), lambda j, f, be, na, nv: (be[jj(j, na)], 0, 0)),
        ],
        out_specs=pl.BlockSpec((tmE, R, LANES), lambda j, f, be, na, nv: (j, 0, 0),
                               pipeline_mode=pl.Buffered(1)),
        scratch_shapes=[pltpu.VMEM((tmE, D), BF16), pltpu.VMEM((tmE, D), F32)],
    )
    return pl.pallas_call(
        _moe_kernel,
        grid_spec=grid_spec,
        out_shape=jax.ShapeDtypeStruct((P, R, LANES), jnp.uint32),
        compiler_params=_cparams(("arbitrary", "arbitrary")),
        name="moe",
    )(blk_e, n_act, n_valid, xs, w_gate, b_gate.reshape(E, 1, Fd), w_up, b_up.reshape(E, 1, Fd),
      w_down, b_down.reshape(E, 1, D))


def _combine_kernel(dest_ref, dnext_ref, x1_ref, gate_ref, g2_ref, fg_ref, ys_ref, o_ref, buf_ref, sems):
    i = pl.program_id(0)
    nsteps = pl.num_programs(0)
    tc = x1_ref.shape[0]
    slot = i % 2

    def row_copy(s, r, k, src):
        return pltpu.make_async_copy(ys_ref.at[src], buf_ref.at[s, k, r], sems.at[s])

    def gather(s, idx_ref):
        def issue(r, carry):
            for k in range(TOP_K):
                row_copy(s, r, k, idx_ref[0, 0, r * TOP_K + k]).start(priority=k % 2)
            return carry

        lax.fori_loop(0, tc, issue, 0, unroll=ROW_DMA_UNROLL)

    @pl.when(i == 0)
    def _():
        gather(0, dest_ref)

    @pl.when(i + 1 < nsteps)
    def _():
        gather(1 - slot, dnext_ref)

    def drain(r, carry):
        for k in range(TOP_K):
            row_copy(slot, 0, 0, 0).wait()
        return carry

    lax.fori_loop(0, tc, drain, 0, unroll=ROW_DMA_UNROLL)

    gates = gate_ref[...]
    moe = jnp.zeros(x1_ref.shape, F32)
    for k in range(TOP_K):
        lo, hi = _unpack_bf16_pairs(_from_row_tiles(buf_ref[slot, k]))
        yk = jnp.concatenate([lo, hi], axis=1)
        moe = moe + yk * gates[:, k:k + 1]
    x2 = x1_ref[...] + g2_ref[0] * moe
    y = x2 * lax.rsqrt(jnp.mean(x2 * x2, axis=-1, keepdims=True) + EPS)
    o_ref[...] = y * fg_ref[...]


def _combine(dest, x1, gates, mods3, final_g, ys, tc, seq_len):
    T, D = x1.shape
    nsteps = T // tc
    dest3 = dest.reshape(nsteps, 1, tc * TOP_K)
    brow = lambda i: (i * tc) // seq_len
    return pl.pallas_call(
        _combine_kernel,
        grid=(nsteps,),
        in_specs=[
            pl.BlockSpec((1, 1, tc * TOP_K), lambda i: (i, 0, 0), memory_space=pltpu.SMEM),
            pl.BlockSpec((1, 1, tc * TOP_K), lambda i: (jnp.minimum(i + 1, nsteps - 1), 0, 0),
                         memory_space=pltpu.SMEM),
            pl.BlockSpec((tc, D), lambda i: (i, 0)),
            pl.BlockSpec((tc, TOP_K), lambda i: (i, 0)),
            pl.BlockSpec((1, 1, D), lambda i: (brow(i), 0, 5)),
            pl.BlockSpec((1, D), lambda i: (0, 0)),
            pl.BlockSpec(memory_space=pl.ANY),
        ],
        out_specs=pl.BlockSpec((tc, D), lambda i: (i, 0)),
        out_shape=jax.ShapeDtypeStruct((T, D), F32),
        scratch_shapes=[pltpu.VMEM((2, TOP_K, tc) + ys.shape[1:], ys.dtype), pltpu.SemaphoreType.DMA((2,))],
        compiler_params=_cparams(("arbitrary",)),
        name="combine",
    )(dest3, dest3, x1, gates, mods3, final_g.reshape(1, D), ys)


def _pick(n, pref):
    t = min(n, pref)
    while n % t:
        t //= 2
    return t


def _rope_tables(L, hd):
    quarter = hd // 4
    pos = np.arange(L)
    inv = ROPE_THETA ** (-jnp.arange(quarter, dtype=F32) / quarter)
    tabs_c, tabs_s = [], []
    for p in (pos // GRID_W, pos % GRID_W):
        ang = jnp.asarray(p, F32)[:, None] * inv[None, :]
        c, s = jnp.cos(ang), jnp.sin(ang)
        tabs_c += [c, c]
        tabs_s += [-s, s]
    return jnp.concatenate(tabs_c, axis=1), jnp.concatenate(tabs_s, axis=1)


def _layer(x, c, ctx, c_ctx, ada_w, ada_b, norm1_g, norm2_g, w_in, conv_w, conv_b, lru_w_a, lru_b_a,
           lru_w_x, lru_b_x, lru_lambda, ret_norm_g, lru_norm_g, w_out, router_w, router_b,
           w_gate, b_gate, w_up, b_up, w_down, b_down, final_norm_g):
    B, L, D = x.shape
    Lc = ctx.shape[1]
    T = B * L
    Dl = conv_w.shape[-1]
    Dr = (w_in.shape[1] - 2 * Dl) // 4
    hd = Dr // RET_HEADS
    E = router_w.shape[1]
    assert Dr == Dl and hd == 2 * LANES and Dl == LRU_BLOCKS * LRU_BD
    cw = Dr

    R = ((B + 1 + SUBLANES - 1) // SUBLANES) * SUBLANES
    cond = jnp.zeros((R, D), F32).at[:B].set(c).at[B].set(c_ctx)
    mods = _ada(cond, ada_w, ada_b, tn=_pick(6 * D, 1536))
    mods3 = mods.reshape(R, 1, 6 * D)

    w_in_bf = w_in.astype(BF16)
    cos_t, sin_t = _rope_tables(L, hd)
    tm = _pick(L, 1024)
    x2d = x.reshape(T, D)
    proj = _in_proj(x2d, mods3, lambda i: (i * tm) // L, norm1_g, w_in_bf, cos_t, sin_t,
                    ("q_rope", "k_rope", "plain", "plain", "plain", "plain"), tm, L, hd)
    tmc = _pick(Lc, 512)
    w_ctx = jnp.concatenate([w_in_bf[:, cw:2 * cw], w_in_bf[:, 2 * cw:3 * cw], w_in_bf[:, 4 * cw:5 * cw]], axis=1)
    proj_c = _in_proj(ctx.reshape(B * Lc, D), mods3, lambda i: B, norm1_g, w_ctx, cos_t, sin_t,
                      ("k_plain", "plain", "plain"), tmc, Lc, hd)

    log_g = jnp.log1p(-(2.0 ** (-5.0 - jnp.arange(RET_HEADS, dtype=F32))))
    ret = _retention(log_g, proj, (0, 1, 2, 3), proj_c, (0, 1), ret_norm_g, B, L, Lc, hd)

    w_cat = (0.5 * jnp.concatenate([lru_w_a, lru_w_x], axis=-1)).astype(BF16)
    tL = _pick(L, 32)
    tLc = _pick(Lc, 32)
    h_zero = jnp.zeros((B, Dl), F32)
    lru_args = lambda d: (conv_w, conv_b, w_cat[d], lru_b_a[d], lru_b_x[d], lru_lambda[d])
    proj3 = proj.reshape(B, L, proj.shape[1])
    proj_c3 = proj_c.reshape(B, Lc, proj_c.shape[1])
    _, h_f = _lru(proj_c3, 2, h_zero, *lru_args(0), reverse=False, tL=tLc)
    _, h_b = _lru(proj_c3, 2, h_zero, *lru_args(1), reverse=True, tL=tLc)
    hf3, _ = _lru(proj3, 4, h_f, *lru_args(0), reverse=False, tL=tL)
    lru, _ = _lru(proj3, 4, h_b, *lru_args(1), reverse=True, tL=tL, merge_args=(hf3, 5, lru_norm_g))
    lru = lru.reshape(T, Dl)

    x1, h2, logits = _out_proj(ret, lru, x2d, mods3, norm2_g, w_out.astype(BF16),
                               router_w.astype(BF16), router_b, _pick(L, 256), L)

    gates, eidx, rank, counts = _route(logits, _pick(T, 1024))
    tmE = _pick(T, 1024)
    counts = counts.reshape(E)
    nblk = (counts + tmE - 1) // tmE
    blk_end = jnp.cumsum(nblk)
    pad_start = (blk_end - nblk) * tmE
    NB = (T * TOP_K) // tmE + E
    n_act = blk_end[-1:].astype(jnp.int32)
    blk_e = jnp.minimum(jnp.sum(blk_end[None, :] <= jnp.arange(NB)[:, None], axis=1), E - 1).astype(jnp.int32)
    zrow = (jnp.maximum(blk_end - 1, 0) * tmE).astype(jnp.int32)
    expert_hot = eidx[..., None] == jnp.arange(E, dtype=jnp.int32)
    dest = (jnp.sum(jnp.where(expert_hot, pad_start.astype(jnp.int32), 0), axis=-1) + rank).astype(jnp.int32)

    blk_in_e = jnp.arange(NB) - (blk_end - nblk)[blk_e]
    n_valid = jnp.where(jnp.arange(NB) < n_act[0],
                        jnp.clip(counts[blk_e] - blk_in_e * tmE, 0, tmE), 0).astype(jnp.int32)

    P = NB * tmE
    xs = _dispatch(h2, dest, zrow, nblk.astype(jnp.int32), n_act, P, tmE, _pick(T, 256))
    ys = _moe(blk_e, n_act, n_valid, xs, w_gate, b_gate, w_up, b_up, w_down, b_down, tmE,
              _pick(w_gate.shape[-1], 512))
    out = _combine(dest, x1, gates, mods3, final_norm_g, ys, _pick(T, 256), L)
    return out.reshape(B, L, D)


def kernel(x, c, ctx, c_ctx, ada_w, ada_b, norm1_g, norm2_g, w_in, conv_w, conv_b, lru_w_a, lru_b_a,
           lru_w_x, lru_b_x, lru_lambda, ret_norm_g, lru_norm_g, w_out, router_w, router_b,
           w_gate, b_gate, w_up, b_up, w_down, b_down, final_norm_g):
    assert ada_w.shape[0] == 1, "single-layer configuration"
    return _layer(x, c, ctx, c_ctx, ada_w[0], ada_b[0], norm1_g[0], norm2_g[0], w_in[0], conv_w[0], conv_b[0],
                  lru_w_a[0], lru_b_a[0], lru_w_x[0], lru_b_x[0], lru_lambda[0], ret_norm_g[0], lru_norm_g[0],
                  w_out[0], router_w[0], router_b[0], w_gate[0], b_gate[0], w_up[0], b_up[0], w_down[0],
                  b_down[0], final_norm_g)
```

```python
import functools

import jax
import jax.numpy as jnp
import numpy as np
from jax import lax
from jax.experimental import pallas as pl
from jax.experimental.pallas import tpu as pltpu

RET_HEADS = 4
LRU_BLOCKS = 8
LRU_BD = 128
CONV_W = 4
GRID_W = 64
TOP_K = 4
LRU_C = 8.0
SWIGLU_LIMIT = 7.0
SWIGLU_ALPHA = 1.702
ROPE_THETA = 10000.0
EPS = 1e-6

LANES = 128
SUBLANES = 8
VMEM_LIMIT = 56 * 1024 * 1024
MOE_VMEM_LIMIT = 60 * 1024 * 1024

F32 = jnp.float32
BF16 = jnp.bfloat16


def _cparams(sem, vmem_limit=VMEM_LIMIT):
    return pltpu.CompilerParams(dimension_semantics=sem, vmem_limit_bytes=vmem_limit)


def _sigmoid(z):
    return 0.5 * jnp.tanh(0.5 * z) + 0.5


def _pack_bf16_pairs(v):
    n = v.shape[1] // 2
    lo = lax.bitcast_convert_type(v[:, :n].astype(F32), jnp.uint32)
    hi = lax.bitcast_convert_type(v[:, n:].astype(F32), jnp.uint32)
    return (lo >> 16) | (hi & jnp.uint32(0xFFFF0000))


def _to_row_tiles(v):
    n = v.shape[1] // LANES
    st = jnp.stack([v[:, j * LANES:(j + 1) * LANES] for j in range(n)], axis=0)
    return pltpu.einshape("jtl->tjl", st)


def _from_row_tiles(v3):
    st = pltpu.einshape("tjl->jtl", v3)
    return jnp.concatenate([st[j] for j in range(st.shape[0])], axis=1)


def _unpack_bf16_pairs(w):
    lo = lax.bitcast_convert_type(w << 16, F32)
    hi = lax.bitcast_convert_type(w & jnp.uint32(0xFFFF0000), F32)
    return lo, hi


def _ada_kernel(c_ref, w_ref, b_ref, o_ref):
    c = c_ref[...]
    s = (c * _sigmoid(c)).astype(BF16)
    o_ref[...] = jnp.dot(s, w_ref[...].astype(BF16), preferred_element_type=F32) + b_ref[...]


def _ada(cond, w, b, tn):
    R, D = cond.shape
    N = w.shape[1]
    return pl.pallas_call(
        _ada_kernel,
        grid=(N // tn,),
        in_specs=[
            pl.BlockSpec((R, D), lambda j: (0, 0)),
            pl.BlockSpec((D, tn), lambda j: (0, j)),
            pl.BlockSpec((1, tn), lambda j: (0, j)),
        ],
        out_specs=pl.BlockSpec((R, tn), lambda j: (0, j)),
        out_shape=jax.ShapeDtypeStruct((R, N), F32),
        compiler_params=_cparams(("arbitrary",)),
        name="ada",
    )(cond, w, b.reshape(1, N))


def _in_proj_kernel(x_ref, sc_ref, sh_ref, g_ref, w_ref, cos_ref, sin_ref, o, xn_ref, *, kinds, hd, scale):
    n = pl.program_id(1)

    @pl.when(n == 0)
    def _():
        x = x_ref[...]
        y = x * lax.rsqrt(jnp.mean(x * x, axis=-1, keepdims=True) + EPS)
        y = y * g_ref[...]
        y = y * (1.0 + sc_ref[0]) + sh_ref[0]
        xn_ref[...] = y.astype(BF16)

    def matmul():
        return jnp.dot(xn_ref[...], w_ref[...], preferred_element_type=F32)

    def rope(a):
        pieces = []
        n_grp = a.shape[1] // LANES
        per_head = hd // LANES
        for gi in range(n_grp):
            sl = a[:, gi * LANES:(gi + 1) * LANES]
            t = gi % per_head
            c = cos_ref[:, t * LANES:(t + 1) * LANES]
            s = sin_ref[:, t * LANES:(t + 1) * LANES]
            pieces.append(sl * c + pltpu.roll(sl, LANES // 2, 1) * s)
        return jnp.concatenate(pieces, axis=1)

    for kind in dict.fromkeys(kinds):
        cond = functools.reduce(jnp.logical_or, [n == idx for idx, kd in enumerate(kinds) if kd == kind])

        @pl.when(cond)
        def _(kind=kind):
            if kind == "q_rope":
                o[...] = rope(matmul()).astype(o.dtype)
            elif kind == "k_rope":
                o[...] = (rope(matmul()) * scale).astype(o.dtype)
            elif kind == "k_plain":
                o[...] = (matmul() * scale).astype(o.dtype)
            else:
                o[...] = matmul().astype(o.dtype)


def _in_proj(x2d, mods3, mod_row_fn, norm_g, w_bf, cos_t, sin_t, kinds, tm, seq_len, hd):
    T, D = x2d.shape
    n_out = len(kinds)
    cw = w_bf.shape[1] // n_out
    tiles_per_seq = seq_len // tm
    in_specs = [
        pl.BlockSpec((tm, D), lambda i, n: (i, 0)),
        pl.BlockSpec((1, 1, D), lambda i, n: (mod_row_fn(i), 0, 1)),
        pl.BlockSpec((1, 1, D), lambda i, n: (mod_row_fn(i), 0, 0)),
        pl.BlockSpec((1, D), lambda i, n: (0, 0)),
        pl.BlockSpec((D, cw), lambda i, n: (0, n)),
        pl.BlockSpec((tm, hd), lambda i, n: (i % tiles_per_seq, 0)),
        pl.BlockSpec((tm, hd), lambda i, n: (i % tiles_per_seq, 0)),
    ]
    return pl.pallas_call(
        functools.partial(_in_proj_kernel, kinds=tuple(kinds), hd=hd, scale=float(hd) ** -0.5),
        grid=(T // tm, n_out),
        in_specs=in_specs,
        out_specs=pl.BlockSpec((tm, cw), lambda i, n: (i, n)),
        out_shape=jax.ShapeDtypeStruct((T, n_out * cw), BF16),
        scratch_shapes=[pltpu.VMEM((tm, D), BF16)],
        compiler_params=_cparams(("arbitrary", "arbitrary")),
        name="in_proj",
    )(x2d, mods3, mods3, norm_g.reshape(1, D), w_bf, cos_t, sin_t)


def _dot_t0(a, b):
    return lax.dot_general(a, b, (((0,), (0,)), ((), ())), preferred_element_type=F32)


def _dot_t1(a, b):
    return lax.dot_general(a, b, (((1,), (1,)), ((), ())), preferred_element_type=F32)


RET_CHUNK = 256
RET_UNROLL = 4


def _ret_kernel(lg_ref, q_ref, k_ref, v_ref, g_ref, kc_ref, vc_ref, gn_ref, o_ref,
                oacc_ref, s_ref, sf_ref, dec_ref, *, C, n_chunks, n_cchunks):
    hd = q_ref.shape[1]
    lg = lg_ref[pl.program_id(0)]

    @pl.when(pl.program_id(1) == 0)
    def _():
        ri = lax.broadcasted_iota(jnp.int32, (C, C), 0).astype(F32)
        ci = lax.broadcasted_iota(jnp.int32, (C, C), 1).astype(F32)
        pos = lax.broadcasted_iota(jnp.int32, (C, hd), 0).astype(F32)
        dec_ref[0] = jnp.exp(jnp.abs(ri - ci) * lg)
        dec_ref[1] = jnp.exp((pos + 1.0) * lg)
        dec_ref[2] = jnp.exp((float(C) - pos) * lg)
        dec_ref[3] = jnp.exp((float(C) - 1.0 - pos) * lg)
        dec_ref[4] = jnp.exp(pos * lg)

    DSYM, QD_F, QD_B, KD_F, KD_B = range(5)
    cd = jnp.exp(jnp.full((1, 1), float(C), F32) * lg)

    def kv_update(s, kk, vv, kd_slot):
        kw = (kk.astype(F32) * dec_ref[kd_slot]).astype(BF16)
        upd = _dot_t0(kw, vv)
        return upd if s is None else s * cd + upd

    s_f = None
    for c in range(n_cchunks):
        s_f = kv_update(s_f, kc_ref[c * C:(c + 1) * C, :], vc_ref[c * C:(c + 1) * C, :], KD_F)
    sf_ref[...] = s_f
    s_b = None
    for c in reversed(range(n_cchunks)):
        s_b = kv_update(s_b, kc_ref[c * C:(c + 1) * C, :], vc_ref[c * C:(c + 1) * C, :], KD_B)

    s_ref[...] = s_b

    def bwd(idx, carry):
        n = n_chunks - 1 - idx
        rows = pl.ds(pl.multiple_of(n * C, C), C)
        s = s_ref[...]
        oacc_ref[rows, :] = jnp.dot(q_ref[rows, :], s.astype(BF16), preferred_element_type=F32) * dec_ref[QD_B]
        s_ref[...] = kv_update(s, k_ref[rows, :], v_ref[rows, :], KD_B)
        return carry

    lax.fori_loop(0, n_chunks, bwd, 0, unroll=RET_UNROLL)

    s_ref[...] = sf_ref[...]

    def fwd(n, carry):
        rows = pl.ds(pl.multiple_of(n * C, C), C)
        s = s_ref[...]
        qn = q_ref[rows, :]
        kn = k_ref[rows, :]
        vn = v_ref[rows, :]
        oacc_ref[rows, :] += jnp.dot(qn, s.astype(BF16), preferred_element_type=F32) * dec_ref[QD_F]
        scores = (_dot_t1(qn, kn) * dec_ref[DSYM]).astype(BF16)
        o = oacc_ref[rows, :] + jnp.dot(scores, vn, preferred_element_type=F32)
        s_ref[...] = kv_update(s, kn, vn, KD_F)
        mu = jnp.mean(o, axis=-1, keepdims=True)
        d = o - mu
        var = jnp.mean(d * d, axis=-1, keepdims=True)
        y = d * lax.rsqrt(var + EPS)
        gate = g_ref[rows, :].astype(F32)
        y = y * gn_ref[...] * (gate * _sigmoid(gate))
        o_ref[rows, :] = y.astype(o_ref.dtype)
        return carry

    lax.fori_loop(0, n_chunks, fwd, 0, unroll=RET_UNROLL)


def _retention(log_g, proj, cols, proj_c, cols_c, gn, B, L, Lc, hd):
    T = proj.shape[0]
    H = gn.shape[0] // hd
    C = RET_CHUNK
    assert C == hd
    col = lambda chunk: (lambda h, b, *_: (b, chunk * H + h))
    grid_spec = pltpu.PrefetchScalarGridSpec(
        num_scalar_prefetch=1,
        grid=(H, B),
        in_specs=[pl.BlockSpec((L, hd), col(c)) for c in cols]
        + [pl.BlockSpec((Lc, hd), col(c)) for c in cols_c]
        + [pl.BlockSpec((1, hd), lambda h, b, *_: (0, h))],
        out_specs=pl.BlockSpec((L, hd), lambda h, b, *_: (b, h)),
        scratch_shapes=[pltpu.VMEM((L, hd), F32), pltpu.VMEM((hd, hd), F32), pltpu.VMEM((hd, hd), F32),
                        pltpu.VMEM((5, C, hd), F32)],
    )
    return pl.pallas_call(
        functools.partial(_ret_kernel, C=C, n_chunks=L // C, n_cchunks=Lc // C),
        grid_spec=grid_spec,
        out_shape=jax.ShapeDtypeStruct((T, H * hd), BF16),
        compiler_params=_cparams(("arbitrary", "arbitrary")),
        name="retention",
    )(log_g, proj, proj, proj, proj, proj_c, proj_c, gn.reshape(1, H * hd))


def _gelu_tanh(y):
    k = 0.7978845608028654
    return (0.5 * y) * (1.0 + jnp.tanh(y * (k + (k * 0.044715) * (y * y))))


def _lru_kernel(*refs, reverse, merge, B, tL, nT):
    if merge:
        (u_ref, up_ref, un_ref, h0_ref, cw_ref, cb_ref, w_ref, ba_ref, bx_ref, lam_ref,
         hf_ref, y_ref, gn_ref, out_ref, hlast_ref, a_ref, b_ref, ho_ref, hc_ref) = refs
    else:
        (u_ref, up_ref, un_ref, h0_ref, cw_ref, cb_ref, w_ref, ba_ref, bx_ref, lam_ref,
         out_ref, hlast_ref, a_ref, b_ref, ho_ref, hc_ref) = refs
    j = pl.program_id(0)
    tile = (nT - 1 - j) if reverse else j
    G = LRU_BLOCKS
    bd = LRU_BD
    Dl = G * bd
    halo = up_ref.shape[1]

    @pl.when(j == 0)
    def _():
        for g in range(G):
            hc_ref[g] = h0_ref[:, g * bd:(g + 1) * bd]

    cur = pltpu.einshape("btd->tbd", u_ref[...].astype(F32)).reshape(tL * B, Dl)
    prev = jnp.where(tile > 0, up_ref[:, halo - 1, :].astype(F32), 0.0)
    nxt = [jnp.where(tile < nT - 1, un_ref[:, t, :].astype(F32), 0.0) for t in range(CONV_W - 2)]
    ext = jnp.concatenate([prev, cur] + nxt, axis=0)

    uc = jnp.zeros((tL * B, Dl), F32) + cb_ref[...]
    for tap in range(CONV_W):
        uc = uc + ext[tap * B:(tap + tL) * B, :] * cw_ref[tap:tap + 1, :]

    lam = lam_ref[...]
    nl = -lam
    softplus = jnp.maximum(nl, 0.0) + jnp.log(1.0 + jnp.exp(-jnp.abs(nl)))
    cfac = -LRU_C * softplus

    for g in range(G):
        sl = slice(g * bd, (g + 1) * bd)
        xg = uc[:, sl]
        half = jnp.dot(xg.astype(BF16), w_ref[g], preferred_element_type=F32)
        tr = jnp.tanh(half[:, :bd] + 0.5 * ba_ref[:, sl])
        ti = jnp.tanh(half[:, bd:] + 0.5 * bx_ref[:, sl])
        hc = 0.5 * cfac[:, sl]
        a = jnp.exp(hc * tr + hc)
        a_ref[g] = a
        b_ref[g] = jnp.sqrt(1.0 - a * a) * ((ti + 1.0) * (0.5 * xg))

    def step(s, hs):
        t = (tL - 1 - s) if reverse else s
        rows = pl.ds(pl.multiple_of(t * B, B), B)
        new = []
        for g in range(G):
            hv = a_ref[g, rows, :] * hs[g] + b_ref[g, rows, :]
            ho_ref[g, rows, :] = hv
            new.append(hv)
        return tuple(new)

    hs = lax.fori_loop(0, tL, step, tuple(hc_ref[g] for g in range(G)), unroll=4)
    for g in range(G):
        hc_ref[g] = hs[g]
        hlast_ref[:, g * bd:(g + 1) * bd] = hs[g]

    def batch_major(g):
        return pltpu.einshape("tbd->btd", ho_ref[g].reshape(tL, B, bd))

    if not merge:
        for g in range(G):
            out_ref[:, :, g * bd:(g + 1) * bd] = batch_major(g).astype(out_ref.dtype)
    else:
        ss = jnp.zeros((B, tL, 1), F32)
        for g in range(G):
            sl = slice(g * bd, (g + 1) * bd)
            hsum = hf_ref[:, :, sl].astype(F32) + batch_major(g)
            z = hsum * _gelu_tanh(y_ref[:, :, sl].astype(F32))
            ho_ref[g] = z.reshape(B * tL, bd)
            ss = ss + jnp.sum(z * z, axis=-1, keepdims=True)
        inv = lax.rsqrt(ss * (1.0 / (G * bd)) + EPS)
        for g in range(G):
            sl = slice(g * bd, (g + 1) * bd)
            z = ho_ref[g].reshape(B, tL, bd)
            out_ref[:, :, sl] = (z * inv * gn_ref[:, sl]).astype(out_ref.dtype)


LRU_HALO = 16


def _lru(proj3, ucol, h0, conv_w, conv_b, w_cat, b_a, b_x, lam, *, reverse, tL, merge_args=None):
    B, L, _ = proj3.shape
    Dl = conv_w.shape[-1]
    nT = L // tL
    merge = merge_args is not None
    tile = (lambda j: nT - 1 - j) if reverse else (lambda j: j)
    hb = tL // LRU_HALO
    n_halo = L // LRU_HALO
    row = lambda a: a.reshape(1, Dl)
    in_specs = [
        pl.BlockSpec((B, tL, Dl), lambda j: (0, tile(j), ucol)),
        pl.BlockSpec((B, LRU_HALO, Dl), lambda j: (0, jnp.maximum(tile(j) * hb - 1, 0), ucol)),
        pl.BlockSpec((B, LRU_HALO, Dl), lambda j: (0, jnp.minimum((tile(j) + 1) * hb, n_halo - 1), ucol)),
        pl.BlockSpec((B, Dl), lambda j: (0, 0)),
        pl.BlockSpec((CONV_W, Dl), lambda j: (0, 0)),
        pl.BlockSpec((1, Dl), lambda j: (0, 0)),
        pl.BlockSpec((LRU_BLOCKS, LRU_BD, 2 * LRU_BD), lambda j: (0, 0, 0)),
        pl.BlockSpec((1, Dl), lambda j: (0, 0)),
        pl.BlockSpec((1, Dl), lambda j: (0, 0)),
        pl.BlockSpec((1, Dl), lambda j: (0, 0)),
    ]
    args = [proj3, proj3, proj3, h0, conv_w, row(conv_b), w_cat, row(b_a), row(b_x), row(lam)]
    if merge:
        hf3, ycol, gn = merge_args
        in_specs += [
            pl.BlockSpec((B, tL, Dl), lambda j: (0, tile(j), 0)),
            pl.BlockSpec((B, tL, Dl), lambda j: (0, tile(j), ycol)),
            pl.BlockSpec((1, Dl), lambda j: (0, 0)),
        ]
        args += [hf3, proj3, row(gn)]
        out_dtype = BF16
    else:
        out_dtype = F32
    out, hlast = pl.pallas_call(
        functools.partial(_lru_kernel, reverse=reverse, merge=merge, B=B, tL=tL, nT=nT),
        grid=(nT,),
        in_specs=in_specs,
        out_specs=[
            pl.BlockSpec((B, tL, Dl), lambda j: (0, tile(j), 0)),
            pl.BlockSpec((B, Dl), lambda j: (0, 0)),
        ],
        out_shape=[jax.ShapeDtypeStruct((B, L, Dl), out_dtype), jax.ShapeDtypeStruct((B, Dl), F32)],
        scratch_shapes=[
            pltpu.VMEM((LRU_BLOCKS, B * tL, LRU_BD), F32),
            pltpu.VMEM((LRU_BLOCKS, B * tL, LRU_BD), F32),
            pltpu.VMEM((LRU_BLOCKS, B * tL, LRU_BD), F32),
            pltpu.VMEM((LRU_BLOCKS, B, LRU_BD), F32),
        ],
        compiler_params=_cparams(("arbitrary",)),
        name="lru_bwd_merge" if merge else ("lru_rev" if reverse else "lru_fwd"),
    )(*args)
    return out, hlast


def _out_proj_kernel(ret_ref, lru_ref, x_ref, g1_ref, sc_ref, sh_ref, ng_ref, wo_ref, rw_ref, rb_ref,
                     x1_ref, h2_ref, lg_ref):
    dr = ret_ref.shape[1]
    mix = jnp.dot(ret_ref[...], wo_ref[0:dr, :], preferred_element_type=F32)
    mix = mix + jnp.dot(lru_ref[...], wo_ref[dr:, :], preferred_element_type=F32)
    x1 = x_ref[...] + g1_ref[0] * mix
    x1_ref[...] = x1
    y = x1 * lax.rsqrt(jnp.mean(x1 * x1, axis=-1, keepdims=True) + EPS)
    y = y * ng_ref[...]
    h2 = (y * (1.0 + sc_ref[0]) + sh_ref[0]).astype(BF16)
    h2_ref[...] = _to_row_tiles(_pack_bf16_pairs(h2))
    lg_ref[...] = jnp.dot(h2, rw_ref[...], preferred_element_type=F32) + rb_ref[...]


def _out_proj(ret, lru, x2d, mods3, norm_g, wo_bf, rw_bf, rb, tm, seq_len):
    T, D = x2d.shape
    E = rw_bf.shape[1]
    Dm = wo_bf.shape[0]
    brow = lambda i: (i * tm) // seq_len
    return pl.pallas_call(
        _out_proj_kernel,
        grid=(T // tm,),
        in_specs=[
            pl.BlockSpec((tm, ret.shape[1]), lambda i: (i, 0)),
            pl.BlockSpec((tm, lru.shape[1]), lambda i: (i, 0)),
            pl.BlockSpec((tm, D), lambda i: (i, 0)),
            pl.BlockSpec((1, 1, D), lambda i: (brow(i), 0, 2)),
            pl.BlockSpec((1, 1, D), lambda i: (brow(i), 0, 4)),
            pl.BlockSpec((1, 1, D), lambda i: (brow(i), 0, 3)),
            pl.BlockSpec((1, D), lambda i: (0, 0)),
            pl.BlockSpec((Dm, D), lambda i: (0, 0)),
            pl.BlockSpec((D, E), lambda i: (0, 0)),
            pl.BlockSpec((1, E), lambda i: (0, 0)),
        ],
        out_specs=[
            pl.BlockSpec((tm, D), lambda i: (i, 0)),
            pl.BlockSpec((tm, D // 2 // LANES, LANES), lambda i: (i, 0, 0)),
            pl.BlockSpec((tm, E), lambda i: (i, 0)),
        ],
        out_shape=[
            jax.ShapeDtypeStruct((T, D), F32),
            jax.ShapeDtypeStruct((T, D // 2 // LANES, LANES), jnp.uint32),
            jax.ShapeDtypeStruct((T, E), F32),
        ],
        compiler_params=_cparams(("arbitrary",)),
        name="out_proj",
    )(ret, lru, x2d, mods3, mods3, mods3, norm_g.reshape(1, D), wo_bf, rw_bf, rb.reshape(1, E))


def _route_kernel(lg_ref, gate_ref, eidx_ref, rank_ref, cnt_ref, carry_ref):
    i = pl.program_id(0)
    tr, E = lg_ref.shape

    @pl.when(i == 0)
    def _():
        carry_ref[...] = jnp.zeros_like(carry_ref)

    work = lg_ref[...]
    lane = lax.broadcasted_iota(jnp.int32, (tr, E), 1).astype(F32)
    vals, idxs, hots = [], [], []
    for _ in range(TOP_K):
        m = jnp.max(work, axis=-1, keepdims=True)
        idx = jnp.min(jnp.where(work == m, lane, float(E)), axis=-1, keepdims=True)
        hot = lane == idx
        vals.append(m)
        idxs.append(idx)
        hots.append(hot)
        work = jnp.where(hot, -jnp.inf, work)
    exps = [jnp.exp(v - vals[0]) for v in vals]
    den = exps[0]
    for e in exps[1:]:
        den = den + e
    onehot = jnp.zeros((tr, E), F32)
    for hot in hots:
        onehot = onehot + hot.astype(F32)
    rr = lax.broadcasted_iota(jnp.int32, (tr, tr), 0)
    cc = lax.broadcasted_iota(jnp.int32, (tr, tr), 1)
    tri = (cc < rr).astype(BF16)
    prefix = jnp.dot(tri, onehot.astype(BF16), preferred_element_type=F32) + carry_ref[...]
    k_lane = lax.broadcasted_iota(jnp.int32, (tr, TOP_K), 1)
    gates = jnp.zeros((tr, TOP_K), F32)
    eidx = jnp.zeros((tr, TOP_K), F32)
    rank = jnp.zeros((tr, TOP_K), F32)
    for k in range(TOP_K):
        rk = jnp.sum(jnp.where(hots[k], prefix, 0.0), axis=-1, keepdims=True)
        gates = jnp.where(k_lane == k, exps[k] / den, gates)
        eidx = jnp.where(k_lane == k, idxs[k], eidx)
        rank = jnp.where(k_lane == k, rk, rank)
    gate_ref[...] = gates
    eidx_ref[...] = eidx.astype(jnp.int32)
    rank_ref[...] = rank.astype(jnp.int32)
    carry_ref[...] = carry_ref[...] + jnp.sum(onehot, axis=0, keepdims=True)
    cnt_ref[...] = carry_ref[...].astype(jnp.int32)


def _route(logits, tr):
    T, E = logits.shape
    return pl.pallas_call(
        _route_kernel,
        grid=(T // tr,),
        in_specs=[pl.BlockSpec((tr, E), lambda i: (i, 0))],
        out_specs=[
            pl.BlockSpec((tr, TOP_K), lambda i: (i, 0)),
            pl.BlockSpec((tr, TOP_K), lambda i: (i, 0)),
            pl.BlockSpec((tr, TOP_K), lambda i: (i, 0)),
            pl.BlockSpec((1, E), lambda i: (0, 0)),
        ],
        out_shape=[
            jax.ShapeDtypeStruct((T, TOP_K), F32),
            jax.ShapeDtypeStruct((T, TOP_K), jnp.int32),
            jax.ShapeDtypeStruct((T, TOP_K), jnp.int32),
            jax.ShapeDtypeStruct((1, E), jnp.int32),
        ],
        scratch_shapes=[pltpu.VMEM((1, E), F32)],
        compiler_params=_cparams(("arbitrary",)),
        name="route",
    )(logits)


ROW_DMA_UNROLL = 8


def _dispatch_kernel(zrow_ref, nblk_ref, na_ref, dest_ref, h_ref, xs_ref, zero_ref, sem, zsem, *, tmE, NB):
    i = pl.program_id(0)
    td = h_ref.shape[0]
    E = zrow_ref.shape[0]

    def zero_copy(row):
        return pltpu.make_async_copy(zero_ref, xs_ref.at[pl.ds(row, tmE)], zsem)

    @pl.when(i == 0)
    def _():
        zero_ref[...] = jnp.zeros(zero_ref.shape, zero_ref.dtype)
        for e in range(E):
            @pl.when(nblk_ref[e] > 0)
            def _(e=e):
                zero_copy(pl.multiple_of(zrow_ref[e], tmE)).start()

        def ztail(j, carry):
            zero_copy(pl.multiple_of(j * tmE, tmE)).start()
            return carry

        lax.fori_loop(na_ref[0], NB, ztail, 0)
        for e in range(E):
            @pl.when(nblk_ref[e] > 0)
            def _(e=e):
                zero_copy(0).wait()

        def zwait(j, carry):
            zero_copy(0).wait()
            return carry

        lax.fori_loop(na_ref[0], NB, zwait, 0)

    def row_copy(r, dst):
        return pltpu.make_async_copy(h_ref.at[r], xs_ref.at[dst], sem)

    def issue(r, carry):
        for k in range(TOP_K):
            row_copy(r, dest_ref[0, 0, r * TOP_K + k]).start(priority=k % 2)
        return carry

    lax.fori_loop(0, td, issue, 0, unroll=ROW_DMA_UNROLL)

    def drain(r, carry):
        for k in range(TOP_K):
            row_copy(0, 0).wait()
        return carry

    lax.fori_loop(0, td, drain, 0, unroll=ROW_DMA_UNROLL)


def _dispatch(h2p, dest, zrow, nblk, n_act, P, tmE, td):
    T, R, _ = h2p.shape
    dest3 = dest.reshape(T // td, 1, td * TOP_K)
    grid_spec = pltpu.PrefetchScalarGridSpec(
        num_scalar_prefetch=3,
        grid=(T // td,),
        in_specs=[
            pl.BlockSpec((1, 1, td * TOP_K), lambda i, *_: (i, 0, 0), memory_space=pltpu.SMEM),
            pl.BlockSpec((td, R, LANES), lambda i, *_: (i, 0, 0)),
        ],
        out_specs=pl.BlockSpec(memory_space=pl.ANY),
        scratch_shapes=[pltpu.VMEM((tmE, R, LANES), h2p.dtype), pltpu.SemaphoreType.DMA(()),
                        pltpu.SemaphoreType.DMA(())],
    )
    return pl.pallas_call(
        functools.partial(_dispatch_kernel, tmE=tmE, NB=P // tmE),
        grid_spec=grid_spec,
        out_shape=jax.ShapeDtypeStruct((P, R, LANES), h2p.dtype),
        compiler_params=_cparams(("arbitrary",)),
        name="dispatch",
    )(zrow, nblk, n_act, dest3, h2p)


MOE_SUB_ROWS = 256


def _moe_kernel(be_ref, na_ref, nv_ref, x_ref, wg_ref, bg_ref, wu_ref, bu_ref, wd_ref, bd_ref, y_ref,
                xb_ref, acc_ref):
    j = pl.program_id(0)
    f = pl.program_id(1)
    tmE = y_ref.shape[0]

    @pl.when(jnp.logical_and(j >= na_ref[0], f == 0))
    def _():
        y_ref[...] = jnp.zeros(y_ref.shape, y_ref.dtype)

    @pl.when(j < na_ref[0])
    def _():
        @pl.when(f == 0)
        def _():
            xw = _from_row_tiles(x_ref[...])
            half = xw.shape[1]
            lo, hi = _unpack_bf16_pairs(xw)
            xb_ref[:, :half] = lo.astype(BF16)
            xb_ref[:, half:] = hi.astype(BF16)
            acc_ref[...] = jnp.zeros(acc_ref.shape, F32) + bd_ref[...]

        valid = nv_ref[j]

        def expert_mlp(rows, wg, wu, wd):
            xb = xb_ref[rows, :]
            gl = jnp.dot(xb, wg, preferred_element_type=F32) + bg_ref[...]
            up = jnp.dot(xb, wu, preferred_element_type=F32) + bu_ref[...]
            gl = jnp.minimum(gl, SWIGLU_LIMIT)
            up = jnp.clip(up, -SWIGLU_LIMIT, SWIGLU_LIMIT)
            act = (up + 1.0) * gl * _sigmoid(SWIGLU_ALPHA * gl)
            acc_ref[rows, :] += jnp.dot(act.astype(BF16), wd, preferred_element_type=F32)

        @pl.when(valid == tmE)
        def _():
            expert_mlp(slice(0, tmE), wg_ref[...].astype(BF16), wu_ref[...].astype(BF16),
                       wd_ref[...].astype(BF16))

        for sb in range(tmE // MOE_SUB_ROWS):
            @pl.when(jnp.logical_and(valid < tmE, sb * MOE_SUB_ROWS < valid))
            def _(sb=sb):
                expert_mlp(slice(sb * MOE_SUB_ROWS, (sb + 1) * MOE_SUB_ROWS), wg_ref[...].astype(BF16),
                           wu_ref[...].astype(BF16), wd_ref[...].astype(BF16))

        @pl.when(f == pl.num_programs(1) - 1)
        def _():
            y_ref[...] = _to_row_tiles(_pack_bf16_pairs(acc_ref[...].astype(BF16)))


def _moe(blk_e, n_act, n_valid, xs, w_gate, b_gate, w_up, b_up, w_down, b_down, tmE, tf):
    P, R, _ = xs.shape
    E, D, Fd = w_gate.shape
    NB = P // tmE
    NF = Fd // tf

    def jj(j, na):
        return jnp.minimum(j, na[0] - 1)

    def ff(j, f, na):
        return jnp.where(j < na[0], f, NF - 1)

    grid_spec = pltpu.PrefetchScalarGridSpec(
        num_scalar_prefetch=3,
        grid=(NB, NF),
        in_specs=[
            pl.BlockSpec((tmE, R, LANES), lambda j, f, be, na, nv: (jj(j, na), 0, 0)),
            pl.BlockSpec((None, D, tf), lambda j, f, be, na, nv: (be[jj(j, na)], 0, ff(j, f, na))),
            pl.BlockSpec((None, 1, tf), lambda j, f, be, na, nv: (be[jj(j, na)], 0, ff(j, f, na))),
            pl.BlockSpec((None, D, tf), lambda j, f, be, na, nv: (be[jj(j, na)], 0, ff(j, f, na))),
            pl.BlockSpec((None, 1, tf), lambda j, f, be, na, nv: (be[jj(j, na)], 0, ff(j, f, na))),
            pl.BlockSpec((None, tf, D), lambda j, f, be, na, nv: (be[jj(j, na)], ff(j, f, na), 0)),
            pl.BlockSpec((None, 1, D), lambda j, f, be, na, nv: (be[jj(j, na)], 0, 0)),
        ],
        out_specs=pl.BlockSpec((tmE, R, LANES), lambda j, f, be, na, nv: (j, 0, 0)),
        scratch_shapes=[pltpu.VMEM((tmE, D), BF16), pltpu.VMEM((tmE, D), F32)],
    )
    return pl.pallas_call(
        _moe_kernel,
        grid_spec=grid_spec,
        out_shape=jax.ShapeDtypeStruct((P, R, LANES), jnp.uint32),
        compiler_params=_cparams(("arbitrary", "arbitrary"), MOE_VMEM_LIMIT),
        name="moe",
    )(blk_e, n_act, n_valid, xs, w_gate, b_gate.reshape(E, 1, Fd), w_up, b_up.reshape(E, 1, Fd),
      w_down, b_down.reshape(E, 1, D))


def _combine_kernel(dest_ref, dnext_ref, x1_ref, gate_ref, g2_ref, fg_ref, ys_ref, o_ref, buf_ref, sems):
    i = pl.program_id(0)
    nsteps = pl.num_programs(0)
    tc = x1_ref.shape[0]
    slot = i % 2

    def row_copy(s, r, k, src):
        return pltpu.make_async_copy(ys_ref.at[src], buf_ref.at[s, k, r], sems.at[s])

    def gather(s, idx_ref):
        def issue(r, carry):
            for k in range(TOP_K):
                row_copy(s, r, k, idx_ref[0, 0, r * TOP_K + k]).start(priority=k % 2)
            return carry

        lax.fori_loop(0, tc, issue, 0, unroll=ROW_DMA_UNROLL)

    @pl.when(i == 0)
    def _():
        gather(0, dest_ref)

    @pl.when(i + 1 < nsteps)
    def _():
        gather(1 - slot, dnext_ref)

    def drain(r, carry):
        for k in range(TOP_K):
            row_copy(slot, 0, 0, 0).wait()
        return carry

    lax.fori_loop(0, tc, drain, 0, unroll=ROW_DMA_UNROLL)

    gates = gate_ref[...]
    moe = jnp.zeros(x1_ref.shape, F32)
    for k in range(TOP_K):
        lo, hi = _unpack_bf16_pairs(_from_row_tiles(buf_ref[slot, k]))
        yk = jnp.concatenate([lo, hi], axis=1)
        moe = moe + yk * gates[:, k:k + 1]
    x2 = x1_ref[...] + g2_ref[0] * moe
    y = x2 * lax.rsqrt(jnp.mean(x2 * x2, axis=-1, keepdims=True) + EPS)
    o_ref[...] = y * fg_ref[...]


def _combine(dest, x1, gates, mods3, final_g, ys, tc, seq_len):
    T, D = x1.shape
    nsteps = T // tc
    dest3 = dest.reshape(nsteps, 1, tc * TOP_K)
    brow = lambda i: (i * tc) // seq_len
    return pl.pallas_call(
        _combine_kernel,
        grid=(nsteps,),
        in_specs=[
            pl.BlockSpec((1, 1, tc * TOP_K), lambda i: (i, 0, 0), memory_space=pltpu.SMEM),
            pl.BlockSpec((1, 1, tc * TOP_K), lambda i: (jnp.minimum(i + 1, nsteps - 1), 0, 0),
                         memory_space=pltpu.SMEM),
            pl.BlockSpec((tc, D), lambda i: (i, 0)),
            pl.BlockSpec((tc, TOP_K), lambda i: (i, 0)),
            pl.BlockSpec((1, 1, D), lambda i: (brow(i), 0, 5)),
            pl.BlockSpec((1, D), lambda i: (0, 0)),
            pl.BlockSpec(memory_space=pl.ANY),
        ],
        out_specs=pl.BlockSpec((tc, D), lambda i: (i, 0)),
        out_shape=jax.ShapeDtypeStruct((T, D), F32),
        scratch_shapes=[pltpu.VMEM((2, TOP_K, tc) + ys.shape[1:], ys.dtype), pltpu.SemaphoreType.DMA((2,))],
        compiler_params=_cparams(("arbitrary",)),
        name="combine",
    )(dest3, dest3, x1, gates, mods3, final_g.reshape(1, D), ys)


def _pick(n, pref):
    t = min(n, pref)
    while n % t:
        t //= 2
    return t


def _rope_tables(L, hd):
    quarter = hd // 4
    pos = np.arange(L)
    inv = ROPE_THETA ** (-jnp.arange(quarter, dtype=F32) / quarter)
    tabs_c, tabs_s = [], []
    for p in (pos // GRID_W, pos % GRID_W):
        ang = jnp.asarray(p, F32)[:, None] * inv[None, :]
        c, s = jnp.cos(ang), jnp.sin(ang)
        tabs_c += [c, c]
        tabs_s += [-s, s]
    return jnp.concatenate(tabs_c, axis=1), jnp.concatenate(tabs_s, axis=1)


def _layer(x, c, ctx, c_ctx, ada_w, ada_b, norm1_g, norm2_g, w_in, conv_w, conv_b, lru_w_a, lru_b_a,
           lru_w_x, lru_b_x, lru_lambda, ret_norm_g, lru_norm_g, w_out, router_w, router_b,
           w_gate, b_gate, w_up, b_up, w_down, b_down, final_norm_g):
    B, L, D = x.shape
    Lc = ctx.shape[1]
    T = B * L
    Dl = conv_w.shape[-1]
    Dr = (w_in.shape[1] - 2 * Dl) // 4
    hd = Dr // RET_HEADS
    E = router_w.shape[1]
    assert Dr == Dl and hd == 2 * LANES and Dl == LRU_BLOCKS * LRU_BD
    cw = Dr

    R = ((B + 1 + SUBLANES - 1) // SUBLANES) * SUBLANES
    cond = jnp.zeros((R, D), F32).at[:B].set(c).at[B].set(c_ctx)
    mods = _ada(cond, ada_w, ada_b, tn=_pick(6 * D, 1536))
    mods3 = mods.reshape(R, 1, 6 * D)

    w_in_bf = w_in.astype(BF16)
    cos_t, sin_t = _rope_tables(L, hd)
    tm = _pick(L, 1024)
    x2d = x.reshape(T, D)
    proj = _in_proj(x2d, mods3, lambda i: (i * tm) // L, norm1_g, w_in_bf, cos_t, sin_t,
                    ("q_rope", "k_rope", "plain", "plain", "plain", "plain"), tm, L, hd)
    tmc = _pick(Lc, 512)
    w_ctx = jnp.concatenate([w_in_bf[:, cw:2 * cw], w_in_bf[:, 2 * cw:3 * cw], w_in_bf[:, 4 * cw:5 * cw]], axis=1)
    proj_c = _in_proj(ctx.reshape(B * Lc, D), mods3, lambda i: B, norm1_g, w_ctx, cos_t, sin_t,
                      ("k_plain", "plain", "plain"), tmc, Lc, hd)

    log_g = jnp.log1p(-(2.0 ** (-5.0 - jnp.arange(RET_HEADS, dtype=F32))))
    ret = _retention(log_g, proj, (0, 1, 2, 3), proj_c, (0, 1), ret_norm_g, B, L, Lc, hd)

    w_cat = (0.5 * jnp.concatenate([lru_w_a, lru_w_x], axis=-1)).astype(BF16)
    tL = _pick(L, 32)
    tLc = _pick(Lc, 32)
    h_zero = jnp.zeros((B, Dl), F32)
    lru_args = lambda d: (conv_w, conv_b, w_cat[d], lru_b_a[d], lru_b_x[d], lru_lambda[d])
    proj3 = proj.reshape(B, L, proj.shape[1])
    proj_c3 = proj_c.reshape(B, Lc, proj_c.shape[1])
    _, h_f = _lru(proj_c3, 2, h_zero, *lru_args(0), reverse=False, tL=tLc)
    _, h_b = _lru(proj_c3, 2, h_zero, *lru_args(1), reverse=True, tL=tLc)
    hf3, _ = _lru(proj3, 4, h_f, *lru_args(0), reverse=False, tL=tL)
    lru, _ = _lru(proj3, 4, h_b, *lru_args(1), reverse=True, tL=tL, merge_args=(hf3, 5, lru_norm_g))
    lru = lru.reshape(T, Dl)

    x1, h2, logits = _out_proj(ret, lru, x2d, mods3, norm2_g, w_out.astype(BF16),
                               router_w.astype(BF16), router_b, _pick(L, 256), L)

    gates, eidx, rank, counts = _route(logits, _pick(T, 1024))
    tmE = _pick(T, 1024)
    counts = counts.reshape(E)
    nblk = (counts + tmE - 1) // tmE
    blk_end = jnp.cumsum(nblk)
    pad_start = (blk_end - nblk) * tmE
    NB = (T * TOP_K) // tmE + E
    n_act = blk_end[-1:].astype(jnp.int32)
    blk_e = jnp.minimum(jnp.sum(blk_end[None, :] <= jnp.arange(NB)[:, None], axis=1), E - 1).astype(jnp.int32)
    zrow = (jnp.maximum(blk_end - 1, 0) * tmE).astype(jnp.int32)
    expert_hot = eidx[..., None] == jnp.arange(E, dtype=jnp.int32)
    dest = (jnp.sum(jnp.where(expert_hot, pad_start.astype(jnp.int32), 0), axis=-1) + rank).astype(jnp.int32)

    blk_in_e = jnp.arange(NB) - (blk_end - nblk)[blk_e]
    n_valid = jnp.where(jnp.arange(NB) < n_act[0],
                        jnp.clip(counts[blk_e] - blk_in_e * tmE, 0, tmE), 0).astype(jnp.int32)

    P = NB * tmE
    xs = _dispatch(h2, dest, zrow, nblk.astype(jnp.int32), n_act, P, tmE, _pick(T, 256))
    ys = _moe(blk_e, n_act, n_valid, xs, w_gate, b_gate, w_up, b_up, w_down, b_down, tmE,
              _pick(w_gate.shape[-1], 512))
    out = _combine(dest, x1, gates, mods3, final_norm_g, ys, _pick(T, 256), L)
    return out.reshape(B, L, D)


def kernel(x, c, ctx, c_ctx, ada_w, ada_b, norm1_g, norm2_g, w_in, conv_w, conv_b, lru_w_a, lru_b_a,
           lru_w_x, lru_b_x, lru_lambda, ret_norm_g, lru_norm_g, w_out, router_w, router_b,
           w_gate, b_gate, w_up, b_up, w_down, b_down, final_norm_g):
    assert ada_w.shape[0] == 1, "single-layer configuration"
    return _layer(x, c, ctx, c_ctx, ada_w[0], ada_b[0], norm1_g[0], norm2_g[0], w_in[0], conv_w[0], conv_b[0],
                  lru_w_a[0], lru_b_a[0], lru_w_x[0], lru_b_x[0], lru_lambda[0], ret_norm_g[0], lru_norm_g[0],
                  w_out[0], router_w[0], router_b[0], w_gate[0], b_gate[0], w_up[0], b_up[0], w_down[0],
                  b_down[0], final_norm_g)
```

```python
import functools

import jax
import jax.numpy as jnp
import numpy as np
from jax import lax
from jax.experimental import pallas as pl
from jax.experimental.pallas import tpu as pltpu

RET_HEADS = 4
LRU_BLOCKS = 8
LRU_BD = 128
CONV_W = 4
GRID_W = 64
TOP_K = 4
LRU_C = 8.0
SWIGLU_LIMIT = 7.0
SWIGLU_ALPHA = 1.702
ROPE_THETA = 10000.0
EPS = 1e-6

LANES = 128
SUBLANES = 8
VMEM_LIMIT = 56 * 1024 * 1024
MOE_VMEM_LIMIT = 60 * 1024 * 1024

F32 = jnp.float32
BF16 = jnp.bfloat16


def _cparams(sem, vmem_limit=VMEM_LIMIT):
    return pltpu.CompilerParams(dimension_semantics=sem, vmem_limit_bytes=vmem_limit)


def _sigmoid(z):
    return 0.5 * jnp.tanh(0.5 * z) + 0.5


def _pack_bf16_pairs(v):
    n = v.shape[1] // 2
    lo = lax.bitcast_convert_type(v[:, :n].astype(F32), jnp.uint32)
    hi = lax.bitcast_convert_type(v[:, n:].astype(F32), jnp.uint32)
    return (lo >> 16) | (hi & jnp.uint32(0xFFFF0000))


def _to_row_tiles(v):
    n = v.shape[1] // LANES
    st = jnp.stack([v[:, j * LANES:(j + 1) * LANES] for j in range(n)], axis=0)
    return pltpu.einshape("jtl->tjl", st)


def _from_row_tiles(v3):
    st = pltpu.einshape("tjl->jtl", v3)
    return jnp.concatenate([st[j] for j in range(st.shape[0])], axis=1)


def _unpack_bf16_pairs(w):
    lo = lax.bitcast_convert_type(w << 16, F32)
    hi = lax.bitcast_convert_type(w & jnp.uint32(0xFFFF0000), F32)
    return lo, hi


def _ada_kernel(c_ref, w_ref, b_ref, o_ref):
    c = c_ref[...]
    s = (c * _sigmoid(c)).astype(BF16)
    o_ref[...] = jnp.dot(s, w_ref[...].astype(BF16), preferred_element_type=F32) + b_ref[...]


def _ada(cond, w, b, tn):
    R, D = cond.shape
    N = w.shape[1]
    return pl.pallas_call(
        _ada_kernel,
        grid=(N // tn,),
        in_specs=[
            pl.BlockSpec((R, D), lambda j: (0, 0)),
            pl.BlockSpec((D, tn), lambda j: (0, j)),
            pl.BlockSpec((1, tn), lambda j: (0, j)),
        ],
        out_specs=pl.BlockSpec((R, tn), lambda j: (0, j)),
        out_shape=jax.ShapeDtypeStruct((R, N), F32),
        compiler_params=_cparams(("arbitrary",)),
        name="ada",
    )(cond, w, b.reshape(1, N))


def _in_proj_kernel(x_ref, sc_ref, sh_ref, g_ref, w_ref, cos_ref, sin_ref, o, xn_ref, *, kinds, hd, scale):
    n = pl.program_id(1)

    @pl.when(n == 0)
    def _():
        x = x_ref[...]
        y = x * lax.rsqrt(jnp.mean(x * x, axis=-1, keepdims=True) + EPS)
        y = y * g_ref[...]
        y = y * (1.0 + sc_ref[0]) + sh_ref[0]
        xn_ref[...] = y.astype(BF16)

    def matmul():
        return jnp.dot(xn_ref[...], w_ref[...], preferred_element_type=F32)

    def rope(a):
        pieces = []
        n_grp = a.shape[1] // LANES
        per_head = hd // LANES
        for gi in range(n_grp):
            sl = a[:, gi * LANES:(gi + 1) * LANES]
            t = gi % per_head
            c = cos_ref[:, t * LANES:(t + 1) * LANES]
            s = sin_ref[:, t * LANES:(t + 1) * LANES]
            pieces.append(sl * c + pltpu.roll(sl, LANES // 2, 1) * s)
        return jnp.concatenate(pieces, axis=1)

    for kind in dict.fromkeys(kinds):
        cond = functools.reduce(jnp.logical_or, [n == idx for idx, kd in enumerate(kinds) if kd == kind])

        @pl.when(cond)
        def _(kind=kind):
            if kind == "q_rope":
                o[...] = rope(matmul()).astype(o.dtype)
            elif kind == "k_rope":
                o[...] = (rope(matmul()) * scale).astype(o.dtype)
            elif kind == "k_plain":
                o[...] = (matmul() * scale).astype(o.dtype)
            else:
                o[...] = matmul().astype(o.dtype)


def _in_proj(x2d, mods3, mod_row_fn, norm_g, w_bf, cos_t, sin_t, kinds, tm, seq_len, hd):
    T, D = x2d.shape
    n_out = len(kinds)
    cw = w_bf.shape[1] // n_out
    tiles_per_seq = seq_len // tm
    in_specs = [
        pl.BlockSpec((tm, D), lambda i, n: (i, 0)),
        pl.BlockSpec((1, 1, D), lambda i, n: (mod_row_fn(i), 0, 1)),
        pl.BlockSpec((1, 1, D), lambda i, n: (mod_row_fn(i), 0, 0)),
        pl.BlockSpec((1, D), lambda i, n: (0, 0)),
        pl.BlockSpec((D, cw), lambda i, n: (0, n)),
        pl.BlockSpec((tm, hd), lambda i, n: (i % tiles_per_seq, 0)),
        pl.BlockSpec((tm, hd), lambda i, n: (i % tiles_per_seq, 0)),
    ]
    return pl.pallas_call(
        functools.partial(_in_proj_kernel, kinds=tuple(kinds), hd=hd, scale=float(hd) ** -0.5),
        grid=(T // tm, n_out),
        in_specs=in_specs,
        out_specs=pl.BlockSpec((tm, cw), lambda i, n: (i, n)),
        out_shape=jax.ShapeDtypeStruct((T, n_out * cw), BF16),
        scratch_shapes=[pltpu.VMEM((tm, D), BF16)],
        compiler_params=_cparams(("arbitrary", "arbitrary")),
        name="in_proj",
    )(x2d, mods3, mods3, norm_g.reshape(1, D), w_bf, cos_t, sin_t)


def _dot_t0(a, b):
    return lax.dot_general(a, b, (((0,), (0,)), ((), ())), preferred_element_type=F32)


def _dot_t1(a, b):
    return lax.dot_general(a, b, (((1,), (1,)), ((), ())), preferred_element_type=F32)


RET_CHUNK = 256
RET_UNROLL = 8


def _ret_kernel(lg_ref, q_ref, k_ref, v_ref, g_ref, kc_ref, vc_ref, gn_ref, o_ref,
                oacc_ref, s_ref, sf_ref, dec_ref, *, C, n_chunks, n_cchunks):
    hd = q_ref.shape[1]
    lg = lg_ref[pl.program_id(0)]

    @pl.when(pl.program_id(1) == 0)
    def _():
        ri = lax.broadcasted_iota(jnp.int32, (C, C), 0).astype(F32)
        ci = lax.broadcasted_iota(jnp.int32, (C, C), 1).astype(F32)
        pos = lax.broadcasted_iota(jnp.int32, (C, hd), 0).astype(F32)
        dec_ref[0] = jnp.exp(jnp.abs(ri - ci) * lg)
        dec_ref[1] = jnp.exp((pos + 1.0) * lg)
        dec_ref[2] = jnp.exp((float(C) - pos) * lg)
        dec_ref[3] = jnp.exp((float(C) - 1.0 - pos) * lg)
        dec_ref[4] = jnp.exp(pos * lg)

    DSYM, QD_F, QD_B, KD_F, KD_B = range(5)
    cd = jnp.exp(jnp.full((1, 1), float(C), F32) * lg)

    def kv_update(s, kk, vv, kd_slot):
        kw = (kk.astype(F32) * dec_ref[kd_slot]).astype(BF16)
        upd = _dot_t0(kw, vv)
        return upd if s is None else s * cd + upd

    s_f = None
    for c in range(n_cchunks):
        s_f = kv_update(s_f, kc_ref[c * C:(c + 1) * C, :], vc_ref[c * C:(c + 1) * C, :], KD_F)
    sf_ref[...] = s_f
    s_b = None
    for c in reversed(range(n_cchunks)):
        s_b = kv_update(s_b, kc_ref[c * C:(c + 1) * C, :], vc_ref[c * C:(c + 1) * C, :], KD_B)

    s_ref[...] = s_b

    def bwd(idx, carry):
        n = n_chunks - 1 - idx
        rows = pl.ds(pl.multiple_of(n * C, C), C)
        s = s_ref[...]
        oacc_ref[rows, :] = jnp.dot(q_ref[rows, :], s.astype(BF16), preferred_element_type=F32) * dec_ref[QD_B]
        s_ref[...] = kv_update(s, k_ref[rows, :], v_ref[rows, :], KD_B)
        return carry

    lax.fori_loop(0, n_chunks, bwd, 0, unroll=RET_UNROLL)

    s_ref[...] = sf_ref[...]

    def fwd(n, carry):
        rows = pl.ds(pl.multiple_of(n * C, C), C)
        s = s_ref[...]
        qn = q_ref[rows, :]
        kn = k_ref[rows, :]
        vn = v_ref[rows, :]
        oacc_ref[rows, :] += jnp.dot(qn, s.astype(BF16), preferred_element_type=F32) * dec_ref[QD_F]
        scores = (_dot_t1(qn, kn) * dec_ref[DSYM]).astype(BF16)
        o = oacc_ref[rows, :] + jnp.dot(scores, vn, preferred_element_type=F32)
        s_ref[...] = kv_update(s, kn, vn, KD_F)
        mu = jnp.mean(o, axis=-1, keepdims=True)
        d = o - mu
        var = jnp.mean(d * d, axis=-1, keepdims=True)
        y = d * lax.rsqrt(var + EPS)
        gate = g_ref[rows, :].astype(F32)
        y = y * gn_ref[...] * (gate * _sigmoid(gate))
        o_ref[rows, :] = y.astype(o_ref.dtype)
        return carry

    lax.fori_loop(0, n_chunks, fwd, 0, unroll=RET_UNROLL)


def _retention(log_g, proj, cols, proj_c, cols_c, gn, B, L, Lc, hd):
    T = proj.shape[0]
    H = gn.shape[0] // hd
    C = RET_CHUNK
    assert C == hd
    col = lambda chunk: (lambda h, b, *_: (b, chunk * H + h))
    grid_spec = pltpu.PrefetchScalarGridSpec(
        num_scalar_prefetch=1,
        grid=(H, B),
        in_specs=[pl.BlockSpec((L, hd), col(c)) for c in cols]
        + [pl.BlockSpec((Lc, hd), col(c)) for c in cols_c]
        + [pl.BlockSpec((1, hd), lambda h, b, *_: (0, h))],
        out_specs=pl.BlockSpec((L, hd), lambda h, b, *_: (b, h)),
        scratch_shapes=[pltpu.VMEM((L, hd), F32), pltpu.VMEM((hd, hd), F32), pltpu.VMEM((hd, hd), F32),
                        pltpu.VMEM((5, C, hd), F32)],
    )
    return pl.pallas_call(
        functools.partial(_ret_kernel, C=C, n_chunks=L // C, n_cchunks=Lc // C),
        grid_spec=grid_spec,
        out_shape=jax.ShapeDtypeStruct((T, H * hd), BF16),
        compiler_params=_cparams(("arbitrary", "arbitrary")),
        name="retention",
    )(log_g, proj, proj, proj, proj, proj_c, proj_c, gn.reshape(1, H * hd))


def _gelu_tanh(y):
    k = 0.7978845608028654
    return (0.5 * y) * (1.0 + jnp.tanh(y * (k + (k * 0.044715) * (y * y))))


def _lru_kernel(*refs, reverse, merge, B, tL, nT):
    if merge:
        (u_ref, up_ref, un_ref, h0_ref, cw_ref, cb_ref, w_ref, ba_ref, bx_ref, lam_ref,
         hf_ref, y_ref, gn_ref, out_ref, hlast_ref, a_ref, b_ref, ho_ref, hc_ref) = refs
    else:
        (u_ref, up_ref, un_ref, h0_ref, cw_ref, cb_ref, w_ref, ba_ref, bx_ref, lam_ref,
         out_ref, hlast_ref, a_ref, b_ref, ho_ref, hc_ref) = refs
    j = pl.program_id(0)
    tile = (nT - 1 - j) if reverse else j
    G = LRU_BLOCKS
    bd = LRU_BD
    Dl = G * bd
    halo = up_ref.shape[1]

    @pl.when(j == 0)
    def _():
        for g in range(G):
            hc_ref[g] = h0_ref[:, g * bd:(g + 1) * bd]

    cur = pltpu.einshape("btd->tbd", u_ref[...].astype(F32)).reshape(tL * B, Dl)
    prev = jnp.where(tile > 0, up_ref[:, halo - 1, :].astype(F32), 0.0)
    nxt = [jnp.where(tile < nT - 1, un_ref[:, t, :].astype(F32), 0.0) for t in range(CONV_W - 2)]
    ext = jnp.concatenate([prev, cur] + nxt, axis=0)

    uc = jnp.zeros((tL * B, Dl), F32) + cb_ref[...]
    for tap in range(CONV_W):
        uc = uc + ext[tap * B:(tap + tL) * B, :] * cw_ref[tap:tap + 1, :]

    lam = lam_ref[...]
    nl = -lam
    softplus = jnp.maximum(nl, 0.0) + jnp.log(1.0 + jnp.exp(-jnp.abs(nl)))
    cfac = -LRU_C * softplus

    for g in range(G):
        sl = slice(g * bd, (g + 1) * bd)
        xg = uc[:, sl]
        half = jnp.dot(xg.astype(BF16), w_ref[g], preferred_element_type=F32)
        tr = jnp.tanh(half[:, :bd] + 0.5 * ba_ref[:, sl])
        ti = jnp.tanh(half[:, bd:] + 0.5 * bx_ref[:, sl])
        hc = 0.5 * cfac[:, sl]
        a = jnp.exp(hc * tr + hc)
        a_ref[g] = a
        b_ref[g] = jnp.sqrt(1.0 - a * a) * ((ti + 1.0) * (0.5 * xg))

    def step(s, hs):
        t = (tL - 1 - s) if reverse else s
        rows = pl.ds(pl.multiple_of(t * B, B), B)
        new = []
        for g in range(G):
            hv = a_ref[g, rows, :] * hs[g] + b_ref[g, rows, :]
            ho_ref[g, rows, :] = hv
            new.append(hv)
        return tuple(new)

    hs = lax.fori_loop(0, tL, step, tuple(hc_ref[g] for g in range(G)), unroll=4)
    for g in range(G):
        hc_ref[g] = hs[g]
        hlast_ref[:, g * bd:(g + 1) * bd] = hs[g]

    def batch_major(g):
        return pltpu.einshape("tbd->btd", ho_ref[g].reshape(tL, B, bd))

    if not merge:
        for g in range(G):
            out_ref[:, :, g * bd:(g + 1) * bd] = batch_major(g).astype(out_ref.dtype)
    else:
        ss = jnp.zeros((B, tL, 1), F32)
        for g in range(G):
            sl = slice(g * bd, (g + 1) * bd)
            hsum = hf_ref[:, :, sl].astype(F32) + batch_major(g)
            z = hsum * _gelu_tanh(y_ref[:, :, sl].astype(F32))
            ho_ref[g] = z.reshape(B * tL, bd)
            ss = ss + jnp.sum(z * z, axis=-1, keepdims=True)
        inv = lax.rsqrt(ss * (1.0 / (G * bd)) + EPS)
        for g in range(G):
            sl = slice(g * bd, (g + 1) * bd)
            z = ho_ref[g].reshape(B, tL, bd)
            out_ref[:, :, sl] = (z * inv * gn_ref[:, sl]).astype(out_ref.dtype)


LRU_HALO = 16


def _lru(proj3, ucol, h0, conv_w, conv_b, w_cat, b_a, b_x, lam, *, reverse, tL, merge_args=None):
    B, L, _ = proj3.shape
    Dl = conv_w.shape[-1]
    nT = L // tL
    merge = merge_args is not None
    tile = (lambda j: nT - 1 - j) if reverse else (lambda j: j)
    hb = tL // LRU_HALO
    n_halo = L // LRU_HALO
    row = lambda a: a.reshape(1, Dl)
    in_specs = [
        pl.BlockSpec((B, tL, Dl), lambda j: (0, tile(j), ucol)),
        pl.BlockSpec((B, LRU_HALO, Dl), lambda j: (0, jnp.maximum(tile(j) * hb - 1, 0), ucol)),
        pl.BlockSpec((B, LRU_HALO, Dl), lambda j: (0, jnp.minimum((tile(j) + 1) * hb, n_halo - 1), ucol)),
        pl.BlockSpec((B, Dl), lambda j: (0, 0)),
        pl.BlockSpec((CONV_W, Dl), lambda j: (0, 0)),
        pl.BlockSpec((1, Dl), lambda j: (0, 0)),
        pl.BlockSpec((LRU_BLOCKS, LRU_BD, 2 * LRU_BD), lambda j: (0, 0, 0)),
        pl.BlockSpec((1, Dl), lambda j: (0, 0)),
        pl.BlockSpec((1, Dl), lambda j: (0, 0)),
        pl.BlockSpec((1, Dl), lambda j: (0, 0)),
    ]
    args = [proj3, proj3, proj3, h0, conv_w, row(conv_b), w_cat, row(b_a), row(b_x), row(lam)]
    if merge:
        hf3, ycol, gn = merge_args
        in_specs += [
            pl.BlockSpec((B, tL, Dl), lambda j: (0, tile(j), 0)),
            pl.BlockSpec((B, tL, Dl), lambda j: (0, tile(j), ycol)),
            pl.BlockSpec((1, Dl), lambda j: (0, 0)),
        ]
        args += [hf3, proj3, row(gn)]
        out_dtype = BF16
    else:
        out_dtype = F32
    out, hlast = pl.pallas_call(
        functools.partial(_lru_kernel, reverse=reverse, merge=merge, B=B, tL=tL, nT=nT),
        grid=(nT,),
        in_specs=in_specs,
        out_specs=[
            pl.BlockSpec((B, tL, Dl), lambda j: (0, tile(j), 0)),
            pl.BlockSpec((B, Dl), lambda j: (0, 0)),
        ],
        out_shape=[jax.ShapeDtypeStruct((B, L, Dl), out_dtype), jax.ShapeDtypeStruct((B, Dl), F32)],
        scratch_shapes=[
            pltpu.VMEM((LRU_BLOCKS, B * tL, LRU_BD), F32),
            pltpu.VMEM((LRU_BLOCKS, B * tL, LRU_BD), F32),
            pltpu.VMEM((LRU_BLOCKS, B * tL, LRU_BD), F32),
            pltpu.VMEM((LRU_BLOCKS, B, LRU_BD), F32),
        ],
        compiler_params=_cparams(("arbitrary",)),
        name="lru_bwd_merge" if merge else ("lru_rev" if reverse else "lru_fwd"),
    )(*args)
    return out, hlast


def _out_proj_kernel(ret_ref, lru_ref, x_ref, g1_ref, sc_ref, sh_ref, ng_ref, wo_ref, rw_ref, rb_ref,
                     x1_ref, h2_ref, lg_ref):
    dr = ret_ref.shape[1]
    mix = jnp.dot(ret_ref[...], wo_ref[0:dr, :], preferred_element_type=F32)
    mix = mix + jnp.dot(lru_ref[...], wo_ref[dr:, :], preferred_element_type=F32)
    x1 = x_ref[...] + g1_ref[0] * mix
    x1_ref[...] = x1
    y = x1 * lax.rsqrt(jnp.mean(x1 * x1, axis=-1, keepdims=True) + EPS)
    y = y * ng_ref[...]
    h2 = (y * (1.0 + sc_ref[0]) + sh_ref[0]).astype(BF16)
    h2_ref[...] = _to_row_tiles(_pack_bf16_pairs(h2))
    lg_ref[...] = jnp.dot(h2, rw_ref[...], preferred_element_type=F32) + rb_ref[...]


def _out_proj(ret, lru, x2d, mods3, norm_g, wo_bf, rw_bf, rb, tm, seq_len):
    T, D = x2d.shape
    E = rw_bf.shape[1]
    Dm = wo_bf.shape[0]
    brow = lambda i: (i * tm) // seq_len
    return pl.pallas_call(
        _out_proj_kernel,
        grid=(T // tm,),
        in_specs=[
            pl.BlockSpec((tm, ret.shape[1]), lambda i: (i, 0)),
            pl.BlockSpec((tm, lru.shape[1]), lambda i: (i, 0)),
            pl.BlockSpec((tm, D), lambda i: (i, 0)),
            pl.BlockSpec((1, 1, D), lambda i: (brow(i), 0, 2)),
            pl.BlockSpec((1, 1, D), lambda i: (brow(i), 0, 4)),
            pl.BlockSpec((1, 1, D), lambda i: (brow(i), 0, 3)),
            pl.BlockSpec((1, D), lambda i: (0, 0)),
            pl.BlockSpec((Dm, D), lambda i: (0, 0)),
            pl.BlockSpec((D, E), lambda i: (0, 0)),
            pl.BlockSpec((1, E), lambda i: (0, 0)),
        ],
        out_specs=[
            pl.BlockSpec((tm, D), lambda i: (i, 0)),
            pl.BlockSpec((tm, D // 2 // LANES, LANES), lambda i: (i, 0, 0)),
            pl.BlockSpec((tm, E), lambda i: (i, 0)),
        ],
        out_shape=[
            jax.ShapeDtypeStruct((T, D), F32),
            jax.ShapeDtypeStruct((T, D // 2 // LANES, LANES), jnp.uint32),
            jax.ShapeDtypeStruct((T, E), F32),
        ],
        compiler_params=_cparams(("arbitrary",)),
        name="out_proj",
    )(ret, lru, x2d, mods3, mods3, mods3, norm_g.reshape(1, D), wo_bf, rw_bf, rb.reshape(1, E))


def _route_kernel(lg_ref, gate_ref, eidx_ref, rank_ref, cnt_ref, carry_ref):
    i = pl.program_id(0)
    tr, E = lg_ref.shape

    @pl.when(i == 0)
    def _():
        carry_ref[...] = jnp.zeros_like(carry_ref)

    work = lg_ref[...]
    lane = lax.broadcasted_iota(jnp.int32, (tr, E), 1).astype(F32)
    vals, idxs, hots = [], [], []
    for _ in range(TOP_K):
        m = jnp.max(work, axis=-1, keepdims=True)
        idx = jnp.min(jnp.where(work == m, lane, float(E)), axis=-1, keepdims=True)
        hot = lane == idx
        vals.append(m)
        idxs.append(idx)
        hots.append(hot)
        work = jnp.where(hot, -jnp.inf, work)
    exps = [jnp.exp(v - vals[0]) for v in vals]
    den = exps[0]
    for e in exps[1:]:
        den = den + e
    onehot = jnp.zeros((tr, E), F32)
    for hot in hots:
        onehot = onehot + hot.astype(F32)
    rr = lax.broadcasted_iota(jnp.int32, (tr, tr), 0)
    cc = lax.broadcasted_iota(jnp.int32, (tr, tr), 1)
    tri = (cc < rr).astype(BF16)
    prefix = jnp.dot(tri, onehot.astype(BF16), preferred_element_type=F32) + carry_ref[...]
    k_lane = lax.broadcasted_iota(jnp.int32, (tr, TOP_K), 1)
    gates = jnp.zeros((tr, TOP_K), F32)
    eidx = jnp.zeros((tr, TOP_K), F32)
    rank = jnp.zeros((tr, TOP_K), F32)
    for k in range(TOP_K):
        rk = jnp.sum(jnp.where(hots[k], prefix, 0.0), axis=-1, keepdims=True)
        gates = jnp.where(k_lane == k, exps[k] / den, gates)
        eidx = jnp.where(k_lane == k, idxs[k], eidx)
        rank = jnp.where(k_lane == k, rk, rank)
    gate_ref[...] = gates
    eidx_ref[...] = eidx.astype(jnp.int32)
    rank_ref[...] = rank.astype(jnp.int32)
    carry_ref[...] = carry_ref[...] + jnp.sum(onehot, axis=0, keepdims=True)
    cnt_ref[...] = carry_ref[...].astype(jnp.int32)


def _route(logits, tr):
    T, E = logits.shape
    return pl.pallas_call(
        _route_kernel,
        grid=(T // tr,),
        in_specs=[pl.BlockSpec((tr, E), lambda i: (i, 0))],
        out_specs=[
            pl.BlockSpec((tr, TOP_K), lambda i: (i, 0)),
            pl.BlockSpec((tr, TOP_K), lambda i: (i, 0)),
            pl.BlockSpec((tr, TOP_K), lambda i: (i, 0)),
            pl.BlockSpec((1, E), lambda i: (0, 0)),
        ],
        out_shape=[
            jax.ShapeDtypeStruct((T, TOP_K), F32),
            jax.ShapeDtypeStruct((T, TOP_K), jnp.int32),
            jax.ShapeDtypeStruct((T, TOP_K), jnp.int32),
            jax.ShapeDtypeStruct((1, E), jnp.int32),
        ],
        scratch_shapes=[pltpu.VMEM((1, E), F32)],
        compiler_params=_cparams(("arbitrary",)),
        name="route",
    )(logits)


ROW_DMA_UNROLL = 8


def _dispatch_kernel(zrow_ref, nblk_ref, na_ref, dest_ref, h_ref, xs_ref, zero_ref, sem, zsem, *, tmE, NB):
    i = pl.program_id(0)
    td = h_ref.shape[0]
    E = zrow_ref.shape[0]

    def zero_copy(row):
        return pltpu.make_async_copy(zero_ref, xs_ref.at[pl.ds(row, tmE)], zsem)

    @pl.when(i == 0)
    def _():
        zero_ref[...] = jnp.zeros(zero_ref.shape, zero_ref.dtype)
        for e in range(E):
            @pl.when(nblk_ref[e] > 0)
            def _(e=e):
                zero_copy(pl.multiple_of(zrow_ref[e], tmE)).start()

        def ztail(j, carry):
            zero_copy(pl.multiple_of(j * tmE, tmE)).start()
            return carry

        lax.fori_loop(na_ref[0], NB, ztail, 0)
        for e in range(E):
            @pl.when(nblk_ref[e] > 0)
            def _(e=e):
                zero_copy(0).wait()

        def zwait(j, carry):
            zero_copy(0).wait()
            return carry

        lax.fori_loop(na_ref[0], NB, zwait, 0)

    def row_copy(r, dst):
        return pltpu.make_async_copy(h_ref.at[r], xs_ref.at[dst], sem)

    def issue(r, carry):
        for k in range(TOP_K):
            row_copy(r, dest_ref[0, 0, r * TOP_K + k]).start(priority=k % 2)
        return carry

    lax.fori_loop(0, td, issue, 0, unroll=ROW_DMA_UNROLL)

    def drain(r, carry):
        for k in range(TOP_K):
            row_copy(0, 0).wait()
        return carry

    lax.fori_loop(0, td, drain, 0, unroll=ROW_DMA_UNROLL)


def _dispatch(h2p, dest, zrow, nblk, n_act, P, tmE, td):
    T, R, _ = h2p.shape
    dest3 = dest.reshape(T // td, 1, td * TOP_K)
    grid_spec = pltpu.PrefetchScalarGridSpec(
        num_scalar_prefetch=3,
        grid=(T // td,),
        in_specs=[
            pl.BlockSpec((1, 1, td * TOP_K), lambda i, *_: (i, 0, 0), memory_space=pltpu.SMEM),
            pl.BlockSpec((td, R, LANES), lambda i, *_: (i, 0, 0)),
        ],
        out_specs=pl.BlockSpec(memory_space=pl.ANY),
        scratch_shapes=[pltpu.VMEM((tmE, R, LANES), h2p.dtype), pltpu.SemaphoreType.DMA(()),
                        pltpu.SemaphoreType.DMA(())],
    )
    return pl.pallas_call(
        functools.partial(_dispatch_kernel, tmE=tmE, NB=P // tmE),
        grid_spec=grid_spec,
        out_shape=jax.ShapeDtypeStruct((P, R, LANES), h2p.dtype),
        compiler_params=_cparams(("arbitrary",)),
        name="dispatch",
    )(zrow, nblk, n_act, dest3, h2p)


MOE_SUB_ROWS = 256


def _moe_kernel(be_ref, na_ref, nv_ref, x_ref, wg_ref, bg_ref, wu_ref, bu_ref, wd_ref, bd_ref, y_ref,
                xb_ref, acc_ref):
    j = pl.program_id(0)
    f = pl.program_id(1)
    tmE = y_ref.shape[0]

    @pl.when(jnp.logical_and(j >= na_ref[0], f == 0))
    def _():
        y_ref[...] = jnp.zeros(y_ref.shape, y_ref.dtype)

    @pl.when(j < na_ref[0])
    def _():
        @pl.when(f == 0)
        def _():
            xw = _from_row_tiles(x_ref[...])
            half = xw.shape[1]
            lo, hi = _unpack_bf16_pairs(xw)
            xb_ref[:, :half] = lo.astype(BF16)
            xb_ref[:, half:] = hi.astype(BF16)
            acc_ref[...] = jnp.zeros(acc_ref.shape, F32) + bd_ref[...]

        valid = nv_ref[j]

        def expert_mlp(rows, wg, wu, wd):
            xb = xb_ref[rows, :]
            gl = jnp.dot(xb, wg, preferred_element_type=F32) + bg_ref[...]
            up = jnp.dot(xb, wu, preferred_element_type=F32) + bu_ref[...]
            gl = jnp.minimum(gl, SWIGLU_LIMIT)
            up = jnp.clip(up, -SWIGLU_LIMIT, SWIGLU_LIMIT)
            act = (up + 1.0) * gl * _sigmoid(SWIGLU_ALPHA * gl)
            acc_ref[rows, :] += jnp.dot(act.astype(BF16), wd, preferred_element_type=F32)

        @pl.when(valid == tmE)
        def _():
            expert_mlp(slice(0, tmE), wg_ref[...].astype(BF16), wu_ref[...].astype(BF16),
                       wd_ref[...].astype(BF16))

        for sb in range(tmE // MOE_SUB_ROWS):
            @pl.when(jnp.logical_and(valid < tmE, sb * MOE_SUB_ROWS < valid))
            def _(sb=sb):
                expert_mlp(slice(sb * MOE_SUB_ROWS, (sb + 1) * MOE_SUB_ROWS), wg_ref[...].astype(BF16),
                           wu_ref[...].astype(BF16), wd_ref[...].astype(BF16))

        @pl.when(f == pl.num_programs(1) - 1)
        def _():
            y_ref[...] = _to_row_tiles(_pack_bf16_pairs(acc_ref[...].astype(BF16)))


def _moe(blk_e, n_act, n_valid, xs, w_gate, b_gate, w_up, b_up, w_down, b_down, tmE, tf):
    P, R, _ = xs.shape
    E, D, Fd = w_gate.shape
    NB = P // tmE
    NF = Fd // tf

    def jj(j, na):
        return jnp.minimum(j, na[0] - 1)

    def ff(j, f, na):
        return jnp.where(j < na[0], f, NF - 1)

    grid_spec = pltpu.PrefetchScalarGridSpec(
        num_scalar_prefetch=3,
        grid=(NB, NF),
        in_specs=[
            pl.BlockSpec((tmE, R, LANES), lambda j, f, be, na, nv: (jj(j, na), 0, 0)),
            pl.BlockSpec((None, D, tf), lambda j, f, be, na, nv: (be[jj(j, na)], 0, ff(j, f, na))),
            pl.BlockSpec((None, 1, tf), lambda j, f, be, na, nv: (be[jj(j, na)], 0, ff(j, f, na))),
            pl.BlockSpec((None, D, tf), lambda j, f, be, na, nv: (be[jj(j, na)], 0, ff(j, f, na))),
            pl.BlockSpec((None, 1, tf), lambda j, f, be, na, nv: (be[jj(j, na)], 0, ff(j, f, na))),
            pl.BlockSpec((None, tf, D), lambda j, f, be, na, nv: (be[jj(j, na)], ff(j, f, na), 0)),
            pl.BlockSpec((None, 1, D), lambda j, f, be, na, nv: (be[jj(j, na)], 0, 0)),
        ],
        out_specs=pl.BlockSpec((tmE, R, LANES), lambda j, f, be, na, nv: (j, 0, 0)),
        scratch_shapes=[pltpu.VMEM((tmE, D), BF16), pltpu.VMEM((tmE, D), F32)],
    )
    return pl.pallas_call(
        _moe_kernel,
        grid_spec=grid_spec,
        out_shape=jax.ShapeDtypeStruct((P, R, LANES), jnp.uint32),
        compiler_params=_cparams(("arbitrary", "arbitrary"), MOE_VMEM_LIMIT),
        name="moe",
    )(blk_e, n_act, n_valid, xs, w_gate, b_gate.reshape(E, 1, Fd), w_up, b_up.reshape(E, 1, Fd),
      w_down, b_down.reshape(E, 1, D))


def _combine_kernel(dest_ref, dnext_ref, x1_ref, gate_ref, g2_ref, fg_ref, ys_ref, o_ref, buf_ref, sems):
    i = pl.program_id(0)
    nsteps = pl.num_programs(0)
    tc = x1_ref.shape[0]
    slot = i % 2

    def row_copy(s, r, k, src):
        return pltpu.make_async_copy(ys_ref.at[src], buf_ref.at[s, k, r], sems.at[s])

    def gather(s, idx_ref):
        def issue(r, carry):
            for k in range(TOP_K):
                row_copy(s, r, k, idx_ref[0, 0, r * TOP_K + k]).start(priority=k % 2)
            return carry

        lax.fori_loop(0, tc, issue, 0, unroll=ROW_DMA_UNROLL)

    @pl.when(i == 0)
    def _():
        gather(0, dest_ref)

    @pl.when(i + 1 < nsteps)
    def _():
        gather(1 - slot, dnext_ref)

    def drain(r, carry):
        for k in range(TOP_K):
            row_copy(slot, 0, 0, 0).wait()
        return carry

    lax.fori_loop(0, tc, drain, 0, unroll=ROW_DMA_UNROLL)

    gates = gate_ref[...]
    moe = jnp.zeros(x1_ref.shape, F32)
    for k in range(TOP_K):
        lo, hi = _unpack_bf16_pairs(_from_row_tiles(buf_ref[slot, k]))
        yk = jnp.concatenate([lo, hi], axis=1)
        moe = moe + yk * gates[:, k:k + 1]
    x2 = x1_ref[...] + g2_ref[0] * moe
    y = x2 * lax.rsqrt(jnp.mean(x2 * x2, axis=-1, keepdims=True) + EPS)
    o_ref[...] = y * fg_ref[...]


def _combine(dest, x1, gates, mods3, final_g, ys, tc, seq_len):
    T, D = x1.shape
    nsteps = T // tc
    dest3 = dest.reshape(nsteps, 1, tc * TOP_K)
    brow = lambda i: (i * tc) // seq_len
    return pl.pallas_call(
        _combine_kernel,
        grid=(nsteps,),
        in_specs=[
            pl.BlockSpec((1, 1, tc * TOP_K), lambda i: (i, 0, 0), memory_space=pltpu.SMEM),
            pl.BlockSpec((1, 1, tc * TOP_K), lambda i: (jnp.minimum(i + 1, nsteps - 1), 0, 0),
                         memory_space=pltpu.SMEM),
            pl.BlockSpec((tc, D), lambda i: (i, 0)),
            pl.BlockSpec((tc, TOP_K), lambda i: (i, 0)),
            pl.BlockSpec((1, 1, D), lambda i: (brow(i), 0, 5)),
            pl.BlockSpec((1, D), lambda i: (0, 0)),
            pl.BlockSpec(memory_space=pl.ANY),
        ],
        out_specs=pl.BlockSpec((tc, D), lambda i: (i, 0)),
        out_shape=jax.ShapeDtypeStruct((T, D), F32),
        scratch_shapes=[pltpu.VMEM((2, TOP_K, tc) + ys.shape[1:], ys.dtype), pltpu.SemaphoreType.DMA((2,))],
        compiler_params=_cparams(("arbitrary",)),
        name="combine",
    )(dest3, dest3, x1, gates, mods3, final_g.reshape(1, D), ys)


def _pick(n, pref):
    t = min(n, pref)
    while n % t:
        t //= 2
    return t


def _rope_tables(L, hd):
    quarter = hd // 4
    pos = np.arange(L)
    inv = ROPE_THETA ** (-jnp.arange(quarter, dtype=F32) / quarter)
    tabs_c, tabs_s = [], []
    for p in (pos // GRID_W, pos % GRID_W):
        ang = jnp.asarray(p, F32)[:, None] * inv[None, :]
        c, s = jnp.cos(ang), jnp.sin(ang)
        tabs_c += [c, c]
        tabs_s += [-s, s]
    return jnp.concatenate(tabs_c, axis=1), jnp.concatenate(tabs_s, axis=1)


def _layer(x, c, ctx, c_ctx, ada_w, ada_b, norm1_g, norm2_g, w_in, conv_w, conv_b, lru_w_a, lru_b_a,
           lru_w_x, lru_b_x, lru_lambda, ret_norm_g, lru_norm_g, w_out, router_w, router_b,
           w_gate, b_gate, w_up, b_up, w_down, b_down, final_norm_g):
    B, L, D = x.shape
    Lc = ctx.shape[1]
    T = B * L
    Dl = conv_w.shape[-1]
    Dr = (w_in.shape[1] - 2 * Dl) // 4
    hd = Dr // RET_HEADS
    E = router_w.shape[1]
    assert Dr == Dl and hd == 2 * LANES and Dl == LRU_BLOCKS * LRU_BD
    cw = Dr

    R = ((B + 1 + SUBLANES - 1) // SUBLANES) * SUBLANES
    cond = jnp.zeros((R, D), F32).at[:B].set(c).at[B].set(c_ctx)
    mods = _ada(cond, ada_w, ada_b, tn=_pick(6 * D, 1536))
    mods3 = mods.reshape(R, 1, 6 * D)

    w_in_bf = w_in.astype(BF16)
    cos_t, sin_t = _rope_tables(L, hd)
    tm = _pick(L, 1024)
    x2d = x.reshape(T, D)
    proj = _in_proj(x2d, mods3, lambda i: (i * tm) // L, norm1_g, w_in_bf, cos_t, sin_t,
                    ("q_rope", "k_rope", "plain", "plain", "plain", "plain"), tm, L, hd)
    tmc = _pick(B * Lc, 1024)
    w_ctx = jnp.concatenate([w_in_bf[:, cw:2 * cw], w_in_bf[:, 2 * cw:3 * cw], w_in_bf[:, 4 * cw:5 * cw]], axis=1)
    proj_c = _in_proj(ctx.reshape(B * Lc, D), mods3, lambda i: B, norm1_g, w_ctx, cos_t, sin_t,
                      ("k_plain", "plain", "plain"), tmc, tmc, hd)

    log_g = jnp.log1p(-(2.0 ** (-5.0 - jnp.arange(RET_HEADS, dtype=F32))))
    ret = _retention(log_g, proj, (0, 1, 2, 3), proj_c, (0, 1), ret_norm_g, B, L, Lc, hd)

    w_cat = (0.5 * jnp.concatenate([lru_w_a, lru_w_x], axis=-1)).astype(BF16)
    tL = _pick(L, 64)
    tLc = _pick(Lc, 64)
    h_zero = jnp.zeros((B, Dl), F32)
    lru_args = lambda d: (conv_w, conv_b, w_cat[d], lru_b_a[d], lru_b_x[d], lru_lambda[d])
    proj3 = proj.reshape(B, L, proj.shape[1])
    proj_c3 = proj_c.reshape(B, Lc, proj_c.shape[1])
    _, h_f = _lru(proj_c3, 2, h_zero, *lru_args(0), reverse=False, tL=tLc)
    _, h_b = _lru(proj_c3, 2, h_zero, *lru_args(1), reverse=True, tL=tLc)
    hf3, _ = _lru(proj3, 4, h_f, *lru_args(0), reverse=False, tL=tL)
    lru, _ = _lru(proj3, 4, h_b, *lru_args(1), reverse=True, tL=tL, merge_args=(hf3, 5, lru_norm_g))
    lru = lru.reshape(T, Dl)

    x1, h2, logits = _out_proj(ret, lru, x2d, mods3, norm2_g, w_out.astype(BF16),
                               router_w.astype(BF16), router_b, _pick(L, 256), L)

    gates, eidx, rank, counts = _route(logits, _pick(T, 1024))
    tmE = _pick(T, 1024)
    counts = counts.reshape(E)
    nblk = (counts + tmE - 1) // tmE
    blk_end = jnp.cumsum(nblk)
    pad_start = (blk_end - nblk) * tmE
    NB = (T * TOP_K) // tmE + E
    n_act = blk_end[-1:].astype(jnp.int32)
    blk_e = jnp.minimum(jnp.sum(blk_end[None, :] <= jnp.arange(NB)[:, None], axis=1), E - 1).astype(jnp.int32)
    zrow = (jnp.maximum(blk_end - 1, 0) * tmE).astype(jnp.int32)
    expert_hot = eidx[..., None] == jnp.arange(E, dtype=jnp.int32)
    dest = (jnp.sum(jnp.where(expert_hot, pad_start.astype(jnp.int32), 0), axis=-1) + rank).astype(jnp.int32)

    blk_in_e = jnp.arange(NB) - (blk_end - nblk)[blk_e]
    n_valid = jnp.where(jnp.arange(NB) < n_act[0],
                        jnp.clip(counts[blk_e] - blk_in_e * tmE, 0, tmE), 0).astype(jnp.int32)

    P = NB * tmE
    xs = _dispatch(h2, dest, zrow, nblk.astype(jnp.int32), n_act, P, tmE, _pick(T, 256))
    ys = _moe(blk_e, n_act, n_valid, xs, w_gate, b_gate, w_up, b_up, w_down, b_down, tmE,
              _pick(w_gate.shape[-1], 512))
    out = _combine(dest, x1, gates, mods3, final_norm_g, ys, _pick(T, 256), L)
    return out.reshape(B, L, D)


def kernel(x, c, ctx, c_ctx, ada_w, ada_b, norm1_g, norm2_g, w_in, conv_w, conv_b, lru_w_a, lru_b_a,
           lru_w_x, lru_b_x, lru_lambda, ret_norm_g, lru_norm_g, w_out, router_w, router_b,
           w_gate, b_gate, w_up, b_up, w_down, b_down, final_norm_g):
    assert ada_w.shape[0] == 1, "single-layer configuration"
    return _layer(x, c, ctx, c_ctx, ada_w[0], ada_b[0], norm1_g[0], norm2_g[0], w_in[0], conv_w[0], conv_b[0],
                  lru_w_a[0], lru_b_a[0], lru_w_x[0], lru_b_x[0], lru_lambda[0], ret_norm_g[0], lru_norm_g[0],
                  w_out[0], router_w[0], router_b[0], w_gate[0], b_gate[0], w_up[0], b_up[0], w_down[0],
                  b_down[0], final_norm_g)
```

```python
import functools

import jax
import jax.numpy as jnp
import numpy as np
from jax import lax
from jax.experimental import pallas as pl
from jax.experimental.pallas import tpu as pltpu

RET_HEADS = 4
LRU_BLOCKS = 8
LRU_BD = 128
CONV_W = 4
GRID_W = 64
TOP_K = 4
LRU_C = 8.0
SWIGLU_LIMIT = 7.0
SWIGLU_ALPHA = 1.702
ROPE_THETA = 10000.0
EPS = 1e-6

LANES = 128
SUBLANES = 8
VMEM_LIMIT = 56 * 1024 * 1024
MOE_VMEM_LIMIT = 60 * 1024 * 1024

F32 = jnp.float32
BF16 = jnp.bfloat16


def _cparams(sem, vmem_limit=VMEM_LIMIT):
    return pltpu.CompilerParams(dimension_semantics=sem, vmem_limit_bytes=vmem_limit)


def _sigmoid(z):
    return 0.5 * jnp.tanh(0.5 * z) + 0.5


def _pack_bf16_pairs(v):
    n = v.shape[1] // 2
    lo = lax.bitcast_convert_type(v[:, :n].astype(F32), jnp.uint32)
    hi = lax.bitcast_convert_type(v[:, n:].astype(F32), jnp.uint32)
    return (lo >> 16) | (hi & jnp.uint32(0xFFFF0000))


def _to_row_tiles(v):
    n = v.shape[1] // LANES
    st = jnp.stack([v[:, j * LANES:(j + 1) * LANES] for j in range(n)], axis=0)
    return pltpu.einshape("jtl->tjl", st)


def _from_row_tiles(v3):
    st = pltpu.einshape("tjl->jtl", v3)
    return jnp.concatenate([st[j] for j in range(st.shape[0])], axis=1)


def _unpack_bf16_pairs(w):
    lo = lax.bitcast_convert_type(w << 16, F32)
    hi = lax.bitcast_convert_type(w & jnp.uint32(0xFFFF0000), F32)
    return lo, hi


def _ada_kernel(c_ref, w_ref, b_ref, o_ref):
    c = c_ref[...]
    s = (c * _sigmoid(c)).astype(BF16)
    o_ref[...] = jnp.dot(s, w_ref[...].astype(BF16), preferred_element_type=F32) + b_ref[...]


def _ada(cond, w, b, tn):
    R, D = cond.shape
    N = w.shape[1]
    return pl.pallas_call(
        _ada_kernel,
        grid=(N // tn,),
        in_specs=[
            pl.BlockSpec((R, D), lambda j: (0, 0)),
            pl.BlockSpec((D, tn), lambda j: (0, j)),
            pl.BlockSpec((1, tn), lambda j: (0, j)),
        ],
        out_specs=pl.BlockSpec((R, tn), lambda j: (0, j)),
        out_shape=jax.ShapeDtypeStruct((R, N), F32),
        compiler_params=_cparams(("arbitrary",)),
        name="ada",
    )(cond, w, b.reshape(1, N))


def _in_proj_kernel(x_ref, sc_ref, sh_ref, g_ref, w_ref, cos_ref, sin_ref, o, xn_ref, *, kinds, hd, scale):
    n = pl.program_id(1)

    @pl.when(n == 0)
    def _():
        x = x_ref[...]
        y = x * lax.rsqrt(jnp.mean(x * x, axis=-1, keepdims=True) + EPS)
        y = y * g_ref[...]
        y = y * (1.0 + sc_ref[0]) + sh_ref[0]
        xn_ref[...] = y.astype(BF16)

    def matmul():
        return jnp.dot(xn_ref[...], w_ref[...], preferred_element_type=F32)

    def rope(a):
        pieces = []
        n_grp = a.shape[1] // LANES
        per_head = hd // LANES
        for gi in range(n_grp):
            sl = a[:, gi * LANES:(gi + 1) * LANES]
            t = gi % per_head
            c = cos_ref[:, t * LANES:(t + 1) * LANES]
            s = sin_ref[:, t * LANES:(t + 1) * LANES]
            pieces.append(sl * c + pltpu.roll(sl, LANES // 2, 1) * s)
        return jnp.concatenate(pieces, axis=1)

    for kind in dict.fromkeys(kinds):
        cond = functools.reduce(jnp.logical_or, [n == idx for idx, kd in enumerate(kinds) if kd == kind])

        @pl.when(cond)
        def _(kind=kind):
            if kind == "q_rope":
                o[...] = rope(matmul()).astype(o.dtype)
            elif kind == "k_rope":
                o[...] = (rope(matmul()) * scale).astype(o.dtype)
            elif kind == "k_plain":
                o[...] = (matmul() * scale).astype(o.dtype)
            else:
                o[...] = matmul().astype(o.dtype)


def _in_proj(x2d, mods3, mod_row_fn, norm_g, w_bf, cos_t, sin_t, kinds, tm, seq_len, hd):
    T, D = x2d.shape
    n_out = len(kinds)
    cw = w_bf.shape[1] // n_out
    tiles_per_seq = seq_len // tm
    in_specs = [
        pl.BlockSpec((tm, D), lambda i, n: (i, 0)),
        pl.BlockSpec((1, 1, D), lambda i, n: (mod_row_fn(i), 0, 1)),
        pl.BlockSpec((1, 1, D), lambda i, n: (mod_row_fn(i), 0, 0)),
        pl.BlockSpec((1, D), lambda i, n: (0, 0)),
        pl.BlockSpec((D, cw), lambda i, n: (0, n)),
        pl.BlockSpec((tm, hd), lambda i, n: (i % tiles_per_seq, 0)),
        pl.BlockSpec((tm, hd), lambda i, n: (i % tiles_per_seq, 0)),
    ]
    return pl.pallas_call(
        functools.partial(_in_proj_kernel, kinds=tuple(kinds), hd=hd, scale=float(hd) ** -0.5),
        grid=(T // tm, n_out),
        in_specs=in_specs,
        out_specs=pl.BlockSpec((tm, cw), lambda i, n: (i, n)),
        out_shape=jax.ShapeDtypeStruct((T, n_out * cw), BF16),
        scratch_shapes=[pltpu.VMEM((tm, D), BF16)],
        compiler_params=_cparams(("arbitrary", "arbitrary")),
        name="in_proj",
    )(x2d, mods3, mods3, norm_g.reshape(1, D), w_bf, cos_t, sin_t)


def _dot_t0(a, b):
    return lax.dot_general(a, b, (((0,), (0,)), ((), ())), preferred_element_type=F32)


def _dot_t1(a, b):
    return lax.dot_general(a, b, (((1,), (1,)), ((), ())), preferred_element_type=F32)


RET_CHUNK = 256
RET_UNROLL = 8


def _ret_kernel(lg_ref, q_ref, k_ref, v_ref, g_ref, kc_ref, vc_ref, gn_ref, o_ref,
                oacc_ref, s_ref, sf_ref, dec_ref, *, C, n_chunks, n_cchunks):
    hd = q_ref.shape[1]
    lg = lg_ref[pl.program_id(0)]

    @pl.when(pl.program_id(1) == 0)
    def _():
        ri = lax.broadcasted_iota(jnp.int32, (C, C), 0).astype(F32)
        ci = lax.broadcasted_iota(jnp.int32, (C, C), 1).astype(F32)
        pos = lax.broadcasted_iota(jnp.int32, (C, hd), 0).astype(F32)
        dec_ref[0] = jnp.exp(jnp.abs(ri - ci) * lg)
        dec_ref[1] = jnp.exp((pos + 1.0) * lg)
        dec_ref[2] = jnp.exp((float(C) - pos) * lg)
        dec_ref[3] = jnp.exp((float(C) - 1.0 - pos) * lg)
        dec_ref[4] = jnp.exp(pos * lg)

    DSYM, QD_F, QD_B, KD_F, KD_B = range(5)
    cd = jnp.exp(jnp.full((1, 1), float(C), F32) * lg)

    def kv_update(s, kk, vv, kd_slot):
        kw = (kk.astype(F32) * dec_ref[kd_slot]).astype(BF16)
        upd = _dot_t0(kw, vv)
        return upd if s is None else s * cd + upd

    s_f = None
    for c in range(n_cchunks):
        s_f = kv_update(s_f, kc_ref[c * C:(c + 1) * C, :], vc_ref[c * C:(c + 1) * C, :], KD_F)
    sf_ref[...] = s_f
    s_b = None
    for c in reversed(range(n_cchunks)):
        s_b = kv_update(s_b, kc_ref[c * C:(c + 1) * C, :], vc_ref[c * C:(c + 1) * C, :], KD_B)

    s_ref[...] = s_b

    def bwd(idx, carry):
        n = n_chunks - 1 - idx
        rows = pl.ds(pl.multiple_of(n * C, C), C)
        s = s_ref[...]
        oacc_ref[rows, :] = jnp.dot(q_ref[rows, :], s.astype(BF16), preferred_element_type=F32) * dec_ref[QD_B]
        s_ref[...] = kv_update(s, k_ref[rows, :], v_ref[rows, :], KD_B)
        return carry

    lax.fori_loop(0, n_chunks, bwd, 0, unroll=RET_UNROLL)

    s_ref[...] = sf_ref[...]

    def fwd(n, carry):
        rows = pl.ds(pl.multiple_of(n * C, C), C)
        s = s_ref[...]
        qn = q_ref[rows, :]
        kn = k_ref[rows, :]
        vn = v_ref[rows, :]
        oacc_ref[rows, :] += jnp.dot(qn, s.astype(BF16), preferred_element_type=F32) * dec_ref[QD_F]
        scores = (_dot_t1(qn, kn) * dec_ref[DSYM]).astype(BF16)
        o = oacc_ref[rows, :] + jnp.dot(scores, vn, preferred_element_type=F32)
        s_ref[...] = kv_update(s, kn, vn, KD_F)
        mu = jnp.mean(o, axis=-1, keepdims=True)
        d = o - mu
        var = jnp.mean(d * d, axis=-1, keepdims=True)
        y = d * lax.rsqrt(var + EPS)
        gate = g_ref[rows, :].astype(F32)
        y = y * gn_ref[...] * (gate * _sigmoid(gate))
        o_ref[rows, :] = y.astype(o_ref.dtype)
        return carry

    lax.fori_loop(0, n_chunks, fwd, 0, unroll=RET_UNROLL)


def _retention(log_g, proj, cols, proj_c, cols_c, gn, B, L, Lc, hd):
    T = proj.shape[0]
    H = gn.shape[0] // hd
    C = RET_CHUNK
    assert C == hd
    col = lambda chunk: (lambda h, b, *_: (b, chunk * H + h))
    grid_spec = pltpu.PrefetchScalarGridSpec(
        num_scalar_prefetch=1,
        grid=(H, B),
        in_specs=[pl.BlockSpec((L, hd), col(c)) for c in cols]
        + [pl.BlockSpec((Lc, hd), col(c)) for c in cols_c]
        + [pl.BlockSpec((1, hd), lambda h, b, *_: (0, h))],
        out_specs=pl.BlockSpec((L, hd), lambda h, b, *_: (b, h)),
        scratch_shapes=[pltpu.VMEM((L, hd), F32), pltpu.VMEM((hd, hd), F32), pltpu.VMEM((hd, hd), F32),
                        pltpu.VMEM((5, C, hd), F32)],
    )
    return pl.pallas_call(
        functools.partial(_ret_kernel, C=C, n_chunks=L // C, n_cchunks=Lc // C),
        grid_spec=grid_spec,
        out_shape=jax.ShapeDtypeStruct((T, H * hd), BF16),
        compiler_params=_cparams(("arbitrary", "arbitrary")),
        name="retention",
    )(log_g, proj, proj, proj, proj, proj_c, proj_c, gn.reshape(1, H * hd))


def _gelu_tanh(y):
    k = 0.7978845608028654
    return (0.5 * y) * (1.0 + jnp.tanh(y * (k + (k * 0.044715) * (y * y))))


def _lru_kernel(*refs, reverse, merge, B, tL, nT):
    if merge:
        (u_ref, up_ref, un_ref, h0_ref, cw_ref, cb_ref, w_ref, ba_ref, bx_ref, lam_ref,
         hf_ref, y_ref, gn_ref, out_ref, hlast_ref, a_ref, b_ref, ho_ref, hc_ref) = refs
    else:
        (u_ref, up_ref, un_ref, h0_ref, cw_ref, cb_ref, w_ref, ba_ref, bx_ref, lam_ref,
         out_ref, hlast_ref, a_ref, b_ref, ho_ref, hc_ref) = refs
    j = pl.program_id(0)
    tile = (nT - 1 - j) if reverse else j
    G = LRU_BLOCKS
    bd = LRU_BD
    Dl = G * bd
    halo = up_ref.shape[1]

    @pl.when(j == 0)
    def _():
        for g in range(G):
            hc_ref[g] = h0_ref[:, g * bd:(g + 1) * bd]

    cur = pltpu.einshape("btd->tbd", u_ref[...].astype(F32)).reshape(tL * B, Dl)
    prev = jnp.where(tile > 0, up_ref[:, halo - 1, :].astype(F32), 0.0)
    nxt = [jnp.where(tile < nT - 1, un_ref[:, t, :].astype(F32), 0.0) for t in range(CONV_W - 2)]
    ext = jnp.concatenate([prev, cur] + nxt, axis=0)

    uc = jnp.zeros((tL * B, Dl), F32) + cb_ref[...]
    for tap in range(CONV_W):
        uc = uc + ext[tap * B:(tap + tL) * B, :] * cw_ref[tap:tap + 1, :]

    lam = lam_ref[...]
    nl = -lam
    softplus = jnp.maximum(nl, 0.0) + jnp.log(1.0 + jnp.exp(-jnp.abs(nl)))
    cfac = -LRU_C * softplus

    for g in range(G):
        sl = slice(g * bd, (g + 1) * bd)
        xg = uc[:, sl]
        half = jnp.dot(xg.astype(BF16), w_ref[g], preferred_element_type=F32)
        tr = jnp.tanh(half[:, :bd] + 0.5 * ba_ref[:, sl])
        ti = jnp.tanh(half[:, bd:] + 0.5 * bx_ref[:, sl])
        hc = 0.5 * cfac[:, sl]
        a = jnp.exp(hc * tr + hc)
        a_ref[g] = a
        b_ref[g] = jnp.sqrt(1.0 - a * a) * ((ti + 1.0) * (0.5 * xg))

    def step(s, hs):
        t = (tL - 1 - s) if reverse else s
        rows = pl.ds(pl.multiple_of(t * B, B), B)
        new = []
        for g in range(G):
            hv = a_ref[g, rows, :] * hs[g] + b_ref[g, rows, :]
            ho_ref[g, rows, :] = hv
            new.append(hv)
        return tuple(new)

    hs = lax.fori_loop(0, tL, step, tuple(hc_ref[g] for g in range(G)), unroll=4)
    for g in range(G):
        hc_ref[g] = hs[g]
        hlast_ref[:, g * bd:(g + 1) * bd] = hs[g]

    def batch_major(g):
        return pltpu.einshape("tbd->btd", ho_ref[g].reshape(tL, B, bd))

    if not merge:
        for g in range(G):
            out_ref[:, :, g * bd:(g + 1) * bd] = batch_major(g).astype(out_ref.dtype)
    else:
        ss = jnp.zeros((B, tL, 1), F32)
        for g in range(G):
            sl = slice(g * bd, (g + 1) * bd)
            hsum = hf_ref[:, :, sl].astype(F32) + batch_major(g)
            z = hsum * _gelu_tanh(y_ref[:, :, sl].astype(F32))
            ho_ref[g] = z.reshape(B * tL, bd)
            ss = ss + jnp.sum(z * z, axis=-1, keepdims=True)
        inv = lax.rsqrt(ss * (1.0 / (G * bd)) + EPS)
        for g in range(G):
            sl = slice(g * bd, (g + 1) * bd)
            z = ho_ref[g].reshape(B, tL, bd)
            out_ref[:, :, sl] = (z * inv * gn_ref[:, sl]).astype(out_ref.dtype)


LRU_HALO = 16


def _lru(proj3, ucol, h0, conv_w, conv_b, w_cat, b_a, b_x, lam, *, reverse, tL, merge_args=None):
    B, L, _ = proj3.shape
    Dl = conv_w.shape[-1]
    nT = L // tL
    merge = merge_args is not None
    tile = (lambda j: nT - 1 - j) if reverse else (lambda j: j)
    hb = tL // LRU_HALO
    n_halo = L // LRU_HALO
    row = lambda a: a.reshape(1, Dl)
    in_specs = [
        pl.BlockSpec((B, tL, Dl), lambda j: (0, tile(j), ucol)),
        pl.BlockSpec((B, LRU_HALO, Dl), lambda j: (0, jnp.maximum(tile(j) * hb - 1, 0), ucol)),
        pl.BlockSpec((B, LRU_HALO, Dl), lambda j: (0, jnp.minimum((tile(j) + 1) * hb, n_halo - 1), ucol)),
        pl.BlockSpec((B, Dl), lambda j: (0, 0)),
        pl.BlockSpec((CONV_W, Dl), lambda j: (0, 0)),
        pl.BlockSpec((1, Dl), lambda j: (0, 0)),
        pl.BlockSpec((LRU_BLOCKS, LRU_BD, 2 * LRU_BD), lambda j: (0, 0, 0)),
        pl.BlockSpec((1, Dl), lambda j: (0, 0)),
        pl.BlockSpec((1, Dl), lambda j: (0, 0)),
        pl.BlockSpec((1, Dl), lambda j: (0, 0)),
    ]
    args = [proj3, proj3, proj3, h0, conv_w, row(conv_b), w_cat, row(b_a), row(b_x), row(lam)]
    if merge:
        hf3, ycol, gn = merge_args
        in_specs += [
            pl.BlockSpec((B, tL, Dl), lambda j: (0, tile(j), 0)),
            pl.BlockSpec((B, tL, Dl), lambda j: (0, tile(j), ycol)),
            pl.BlockSpec((1, Dl), lambda j: (0, 0)),
        ]
        args += [hf3, proj3, row(gn)]
        out_dtype = BF16
    else:
        out_dtype = F32
    out, hlast = pl.pallas_call(
        functools.partial(_lru_kernel, reverse=reverse, merge=merge, B=B, tL=tL, nT=nT),
        grid=(nT,),
        in_specs=in_specs,
        out_specs=[
            pl.BlockSpec((B, tL, Dl), lambda j: (0, tile(j), 0)),
            pl.BlockSpec((B, Dl), lambda j: (0, 0)),
        ],
        out_shape=[jax.ShapeDtypeStruct((B, L, Dl), out_dtype), jax.ShapeDtypeStruct((B, Dl), F32)],
        scratch_shapes=[
            pltpu.VMEM((LRU_BLOCKS, B * tL, LRU_BD), F32),
            pltpu.VMEM((LRU_BLOCKS, B * tL, LRU_BD), F32),
            pltpu.VMEM((LRU_BLOCKS, B * tL, LRU_BD), F32),
            pltpu.VMEM((LRU_BLOCKS, B, LRU_BD), F32),
        ],
        compiler_params=_cparams(("arbitrary",)),
        name="lru_bwd_merge" if merge else ("lru_rev" if reverse else "lru_fwd"),
    )(*args)
    return out, hlast


def _out_proj_kernel(ret_ref, lru_ref, x_ref, g1_ref, sc_ref, sh_ref, ng_ref, wo_ref, rw_ref, rb_ref,
                     x1_ref, h2_ref, lg_ref):
    dr = ret_ref.shape[1]
    mix = jnp.dot(ret_ref[...], wo_ref[0:dr, :], preferred_element_type=F32)
    mix = mix + jnp.dot(lru_ref[...], wo_ref[dr:, :], preferred_element_type=F32)
    x1 = x_ref[...] + g1_ref[0] * mix
    x1_ref[...] = x1
    y = x1 * lax.rsqrt(jnp.mean(x1 * x1, axis=-1, keepdims=True) + EPS)
    y = y * ng_ref[...]
    h2 = (y * (1.0 + sc_ref[0]) + sh_ref[0]).astype(BF16)
    h2_ref[...] = _to_row_tiles(_pack_bf16_pairs(h2))
    lg_ref[...] = jnp.dot(h2, rw_ref[...], preferred_element_type=F32) + rb_ref[...]


def _out_proj(ret, lru, x2d, mods3, norm_g, wo_bf, rw_bf, rb, tm, seq_len):
    T, D = x2d.shape
    E = rw_bf.shape[1]
    Dm = wo_bf.shape[0]
    brow = lambda i: (i * tm) // seq_len
    return pl.pallas_call(
        _out_proj_kernel,
        grid=(T // tm,),
        in_specs=[
            pl.BlockSpec((tm, ret.shape[1]), lambda i: (i, 0)),
            pl.BlockSpec((tm, lru.shape[1]), lambda i: (i, 0)),
            pl.BlockSpec((tm, D), lambda i: (i, 0)),
            pl.BlockSpec((1, 1, D), lambda i: (brow(i), 0, 2)),
            pl.BlockSpec((1, 1, D), lambda i: (brow(i), 0, 4)),
            pl.BlockSpec((1, 1, D), lambda i: (brow(i), 0, 3)),
            pl.BlockSpec((1, D), lambda i: (0, 0)),
            pl.BlockSpec((Dm, D), lambda i: (0, 0)),
            pl.BlockSpec((D, E), lambda i: (0, 0)),
            pl.BlockSpec((1, E), lambda i: (0, 0)),
        ],
        out_specs=[
            pl.BlockSpec((tm, D), lambda i: (i, 0)),
            pl.BlockSpec((tm, D // 2 // LANES, LANES), lambda i: (i, 0, 0)),
            pl.BlockSpec((tm, E), lambda i: (i, 0)),
        ],
        out_shape=[
            jax.ShapeDtypeStruct((T, D), F32),
            jax.ShapeDtypeStruct((T, D // 2 // LANES, LANES), jnp.uint32),
            jax.ShapeDtypeStruct((T, E), F32),
        ],
        compiler_params=_cparams(("arbitrary",)),
        name="out_proj",
    )(ret, lru, x2d, mods3, mods3, mods3, norm_g.reshape(1, D), wo_bf, rw_bf, rb.reshape(1, E))


def _route_kernel(lg_ref, gate_ref, eidx_ref, rank_ref, cnt_ref, carry_ref):
    i = pl.program_id(0)
    tr, E = lg_ref.shape

    @pl.when(i == 0)
    def _():
        carry_ref[...] = jnp.zeros_like(carry_ref)

    work = lg_ref[...]
    lane = lax.broadcasted_iota(jnp.int32, (tr, E), 1).astype(F32)
    vals, idxs, hots = [], [], []
    for _ in range(TOP_K):
        m = jnp.max(work, axis=-1, keepdims=True)
        idx = jnp.min(jnp.where(work == m, lane, float(E)), axis=-1, keepdims=True)
        hot = lane == idx
        vals.append(m)
        idxs.append(idx)
        hots.append(hot)
        work = jnp.where(hot, -jnp.inf, work)
    exps = [jnp.exp(v - vals[0]) for v in vals]
    den = exps[0]
    for e in exps[1:]:
        den = den + e
    onehot = jnp.zeros((tr, E), F32)
    for hot in hots:
        onehot = onehot + hot.astype(F32)
    rr = lax.broadcasted_iota(jnp.int32, (tr, tr), 0)
    cc = lax.broadcasted_iota(jnp.int32, (tr, tr), 1)
    tri = (cc < rr).astype(BF16)
    prefix = jnp.dot(tri, onehot.astype(BF16), preferred_element_type=F32) + carry_ref[...]
    k_lane = lax.broadcasted_iota(jnp.int32, (tr, TOP_K), 1)
    gates = jnp.zeros((tr, TOP_K), F32)
    eidx = jnp.zeros((tr, TOP_K), F32)
    rank = jnp.zeros((tr, TOP_K), F32)
    for k in range(TOP_K):
        rk = jnp.sum(jnp.where(hots[k], prefix, 0.0), axis=-1, keepdims=True)
        gates = jnp.where(k_lane == k, exps[k] / den, gates)
        eidx = jnp.where(k_lane == k, idxs[k], eidx)
        rank = jnp.where(k_lane == k, rk, rank)
    gate_ref[...] = gates
    eidx_ref[...] = eidx.astype(jnp.int32)
    rank_ref[...] = rank.astype(jnp.int32)
    carry_ref[...] = carry_ref[...] + jnp.sum(onehot, axis=0, keepdims=True)
    cnt_ref[...] = carry_ref[...].astype(jnp.int32)


def _route(logits, tr):
    T, E = logits.shape
    return pl.pallas_call(
        _route_kernel,
        grid=(T // tr,),
        in_specs=[pl.BlockSpec((tr, E), lambda i: (i, 0))],
        out_specs=[
            pl.BlockSpec((tr, TOP_K), lambda i: (i, 0)),
            pl.BlockSpec((tr, TOP_K), lambda i: (i, 0)),
            pl.BlockSpec((tr, TOP_K), lambda i: (i, 0)),
            pl.BlockSpec((1, E), lambda i: (0, 0)),
        ],
        out_shape=[
            jax.ShapeDtypeStruct((T, TOP_K), F32),
            jax.ShapeDtypeStruct((T, TOP_K), jnp.int32),
            jax.ShapeDtypeStruct((T, TOP_K), jnp.int32),
            jax.ShapeDtypeStruct((1, E), jnp.int32),
        ],
        scratch_shapes=[pltpu.VMEM((1, E), F32)],
        compiler_params=_cparams(("arbitrary",)),
        name="route",
    )(logits)


ROW_DMA_UNROLL = 8


def _dispatch_kernel(zrow_ref, nblk_ref, na_ref, dest_ref, h_ref, xs_ref, zero_ref, sem, zsem, *, tmE, NB):
    i = pl.program_id(0)
    td = h_ref.shape[0]
    E = zrow_ref.shape[0]

    def zero_copy(row):
        return pltpu.make_async_copy(zero_ref, xs_ref.at[pl.ds(row, tmE)], zsem)

    @pl.when(i == 0)
    def _():
        zero_ref[...] = jnp.zeros(zero_ref.shape, zero_ref.dtype)
        for e in range(E):
            @pl.when(nblk_ref[e] > 0)
            def _(e=e):
                zero_copy(pl.multiple_of(zrow_ref[e], tmE)).start()

        def ztail(j, carry):
            zero_copy(pl.multiple_of(j * tmE, tmE)).start()
            return carry

        lax.fori_loop(na_ref[0], NB, ztail, 0)
        for e in range(E):
            @pl.when(nblk_ref[e] > 0)
            def _(e=e):
                zero_copy(0).wait()

        def zwait(j, carry):
            zero_copy(0).wait()
            return carry

        lax.fori_loop(na_ref[0], NB, zwait, 0)

    def row_copy(r, dst):
        return pltpu.make_async_copy(h_ref.at[r], xs_ref.at[dst], sem)

    def issue(r, carry):
        for k in range(TOP_K):
            row_copy(r, dest_ref[0, 0, r * TOP_K + k]).start(priority=k % 2)
        return carry

    lax.fori_loop(0, td, issue, 0, unroll=ROW_DMA_UNROLL)

    def drain(r, carry):
        for k in range(TOP_K):
            row_copy(0, 0).wait()
        return carry

    lax.fori_loop(0, td, drain, 0, unroll=ROW_DMA_UNROLL)


def _dispatch(h2p, dest, zrow, nblk, n_act, P, tmE, td):
    T, R, _ = h2p.shape
    dest3 = dest.reshape(T // td, 1, td * TOP_K)
    grid_spec = pltpu.PrefetchScalarGridSpec(
        num_scalar_prefetch=3,
        grid=(T // td,),
        in_specs=[
            pl.BlockSpec((1, 1, td * TOP_K), lambda i, *_: (i, 0, 0), memory_space=pltpu.SMEM),
            pl.BlockSpec((td, R, LANES), lambda i, *_: (i, 0, 0)),
        ],
        out_specs=pl.BlockSpec(memory_space=pl.ANY),
        scratch_shapes=[pltpu.VMEM((tmE, R, LANES), h2p.dtype), pltpu.SemaphoreType.DMA(()),
                        pltpu.SemaphoreType.DMA(())],
    )
    return pl.pallas_call(
        functools.partial(_dispatch_kernel, tmE=tmE, NB=P // tmE),
        grid_spec=grid_spec,
        out_shape=jax.ShapeDtypeStruct((P, R, LANES), h2p.dtype),
        compiler_params=_cparams(("arbitrary",)),
        name="dispatch",
    )(zrow, nblk, n_act, dest3, h2p)


MOE_SUB_ROWS = 256


def _moe_kernel(be_ref, na_ref, nv_ref, x_ref, wg_ref, bg_ref, wu_ref, bu_ref, wd_ref, bd_ref, y_ref,
                xb_ref, acc_ref):
    j = pl.program_id(0)
    f = pl.program_id(1)
    tmE = y_ref.shape[0]

    @pl.when(jnp.logical_and(j >= na_ref[0], f == 0))
    def _():
        y_ref[...] = jnp.zeros(y_ref.shape, y_ref.dtype)

    @pl.when(j < na_ref[0])
    def _():
        @pl.when(f == 0)
        def _():
            xw = _from_row_tiles(x_ref[...])
            half = xw.shape[1]
            lo, hi = _unpack_bf16_pairs(xw)
            xb_ref[:, :half] = lo.astype(BF16)
            xb_ref[:, half:] = hi.astype(BF16)

        valid = nv_ref[j]

        def expert_mlp(rows, first=False):
            xb = xb_ref[rows, :]
            gl = jnp.dot(xb, wg_ref[...].astype(BF16), preferred_element_type=F32) + bg_ref[...]
            up = jnp.dot(xb, wu_ref[...].astype(BF16), preferred_element_type=F32) + bu_ref[...]
            gl = jnp.minimum(gl, SWIGLU_LIMIT)
            up = jnp.clip(up, -SWIGLU_LIMIT, SWIGLU_LIMIT)
            act = (up + 1.0) * gl * _sigmoid(SWIGLU_ALPHA * gl)
            upd = jnp.dot(act.astype(BF16), wd_ref[...].astype(BF16), preferred_element_type=F32)
            if first:
                acc_ref[rows, :] = upd + bd_ref[...]
            else:
                acc_ref[rows, :] += upd

        @pl.when(jnp.logical_and(valid == tmE, f == 0))
        def _():
            expert_mlp(slice(0, tmE), first=True)

        @pl.when(jnp.logical_and(valid == tmE, f > 0))
        def _():
            expert_mlp(slice(0, tmE))

        @pl.when(jnp.logical_and(valid < tmE, f == 0))
        def _():
            acc_ref[...] = jnp.zeros(acc_ref.shape, F32) + bd_ref[...]

        for sb in range(tmE // MOE_SUB_ROWS):
            @pl.when(jnp.logical_and(valid < tmE, sb * MOE_SUB_ROWS < valid))
            def _(sb=sb):
                expert_mlp(slice(sb * MOE_SUB_ROWS, (sb + 1) * MOE_SUB_ROWS))

        @pl.when(f == pl.num_programs(1) - 1)
        def _():
            y_ref[...] = _to_row_tiles(_pack_bf16_pairs(acc_ref[...].astype(BF16)))


def _moe(blk_e, n_act, n_valid, xs, w_gate, b_gate, w_up, b_up, w_down, b_down, tmE, tf):
    P, R, _ = xs.shape
    E, D, Fd = w_gate.shape
    NB = P // tmE
    NF = Fd // tf

    def jj(j, na):
        return jnp.minimum(j, na[0] - 1)

    def ff(j, f, na):
        return jnp.where(j < na[0], f, NF - 1)

    grid_spec = pltpu.PrefetchScalarGridSpec(
        num_scalar_prefetch=3,
        grid=(NB, NF),
        in_specs=[
            pl.BlockSpec((tmE, R, LANES), lambda j, f, be, na, nv: (jj(j, na), 0, 0)),
            pl.BlockSpec((None, D, tf), lambda j, f, be, na, nv: (be[jj(j, na)], 0, ff(j, f, na))),
            pl.BlockSpec((None, 1, tf), lambda j, f, be, na, nv: (be[jj(j, na)], 0, ff(j, f, na))),
            pl.BlockSpec((None, D, tf), lambda j, f, be, na, nv: (be[jj(j, na)], 0, ff(j, f, na))),
            pl.BlockSpec((None, 1, tf), lambda j, f, be, na, nv: (be[jj(j, na)], 0, ff(j, f, na))),
            pl.BlockSpec((None, tf, D), lambda j, f, be, na, nv: (be[jj(j, na)], ff(j, f, na), 0)),
            pl.BlockSpec((None, 1, D), lambda j, f, be, na, nv: (be[jj(j, na)], 0, 0)),
        ],
        out_specs=pl.BlockSpec((tmE, R, LANES), lambda j, f, be, na, nv: (j, 0, 0)),
        scratch_shapes=[pltpu.VMEM((tmE, D), BF16), pltpu.VMEM((tmE, D), F32)],
    )
    return pl.pallas_call(
        _moe_kernel,
        grid_spec=grid_spec,
        out_shape=jax.ShapeDtypeStruct((P, R, LANES), jnp.uint32),
        compiler_params=_cparams(("arbitrary", "arbitrary"), MOE_VMEM_LIMIT),
        name="moe",
    )(blk_e, n_act, n_valid, xs, w_gate, b_gate.reshape(E, 1, Fd), w_up, b_up.reshape(E, 1, Fd),
      w_down, b_down.reshape(E, 1, D))


def _combine_kernel(dest_ref, dnext_ref, x1_ref, gate_ref, g2_ref, fg_ref, ys_ref, o_ref, buf_ref, sems):
    i = pl.program_id(0)
    nsteps = pl.num_programs(0)
    tc = x1_ref.shape[0]
    slot = i % 2

    def row_copy(s, r, k, src):
        return pltpu.make_async_copy(ys_ref.at[src], buf_ref.at[s, k, r], sems.at[s])

    def gather(s, idx_ref):
        def issue(r, carry):
            for k in range(TOP_K):
                row_copy(s, r, k, idx_ref[0, 0, r * TOP_K + k]).start(priority=k % 2)
            return carry

        lax.fori_loop(0, tc, issue, 0, unroll=ROW_DMA_UNROLL)

    @pl.when(i == 0)
    def _():
        gather(0, dest_ref)

    @pl.when(i + 1 < nsteps)
    def _():
        gather(1 - slot, dnext_ref)

    def drain(r, carry):
        for k in range(TOP_K):
            row_copy(slot, 0, 0, 0).wait()
        return carry

    lax.fori_loop(0, tc, drain, 0, unroll=ROW_DMA_UNROLL)

    gates = gate_ref[...]
    moe = jnp.zeros(x1_ref.shape, F32)
    for k in range(TOP_K):
        lo, hi = _unpack_bf16_pairs(_from_row_tiles(buf_ref[slot, k]))
        yk = jnp.concatenate([lo, hi], axis=1)
        moe = moe + yk * gates[:, k:k + 1]
    x2 = x1_ref[...] + g2_ref[0] * moe
    y = x2 * lax.rsqrt(jnp.mean(x2 * x2, axis=-1, keepdims=True) + EPS)
    o_ref[...] = y * fg_ref[...]


def _combine(dest, x1, gates, mods3, final_g, ys, tc, seq_len):
    T, D = x1.shape
    nsteps = T // tc
    dest3 = dest.reshape(nsteps, 1, tc * TOP_K)
    brow = lambda i: (i * tc) // seq_len
    return pl.pallas_call(
        _combine_kernel,
        grid=(nsteps,),
        in_specs=[
            pl.BlockSpec((1, 1, tc * TOP_K), lambda i: (i, 0, 0), memory_space=pltpu.SMEM),
            pl.BlockSpec((1, 1, tc * TOP_K), lambda i: (jnp.minimum(i + 1, nsteps - 1), 0, 0),
                         memory_space=pltpu.SMEM),
            pl.BlockSpec((tc, D), lambda i: (i, 0)),
            pl.BlockSpec((tc, TOP_K), lambda i: (i, 0)),
            pl.BlockSpec((1, 1, D), lambda i: (brow(i), 0, 5)),
            pl.BlockSpec((1, D), lambda i: (0, 0)),
            pl.BlockSpec(memory_space=pl.ANY),
        ],
        out_specs=pl.BlockSpec((tc, D), lambda i: (i, 0)),
        out_shape=jax.ShapeDtypeStruct((T, D), F32),
        scratch_shapes=[pltpu.VMEM((2, TOP_K, tc) + ys.shape[1:], ys.dtype), pltpu.SemaphoreType.DMA((2,))],
        compiler_params=_cparams(("arbitrary",)),
        name="combine",
    )(dest3, dest3, x1, gates, mods3, final_g.reshape(1, D), ys)


def _pick(n, pref):
    t = min(n, pref)
    while n % t:
        t //= 2
    return t


def _rope_tables(L, hd):
    quarter = hd // 4
    pos = np.arange(L)
    inv = ROPE_THETA ** (-jnp.arange(quarter, dtype=F32) / quarter)
    tabs_c, tabs_s = [], []
    for p in (pos // GRID_W, pos % GRID_W):
        ang = jnp.asarray(p, F32)[:, None] * inv[None, :]
        c, s = jnp.cos(ang), jnp.sin(ang)
        tabs_c += [c, c]
        tabs_s += [-s, s]
    return jnp.concatenate(tabs_c, axis=1), jnp.concatenate(tabs_s, axis=1)


def _layer(x, c, ctx, c_ctx, ada_w, ada_b, norm1_g, norm2_g, w_in, conv_w, conv_b, lru_w_a, lru_b_a,
           lru_w_x, lru_b_x, lru_lambda, ret_norm_g, lru_norm_g, w_out, router_w, router_b,
           w_gate, b_gate, w_up, b_up, w_down, b_down, final_norm_g):
    B, L, D = x.shape
    Lc = ctx.shape[1]
    T = B * L
    Dl = conv_w.shape[-1]
    Dr = (w_in.shape[1] - 2 * Dl) // 4
    hd = Dr // RET_HEADS
    E = router_w.shape[1]
    assert Dr == Dl and hd == 2 * LANES and Dl == LRU_BLOCKS * LRU_BD
    cw = Dr

    R = ((B + 1 + SUBLANES - 1) // SUBLANES) * SUBLANES
    cond = jnp.zeros((R, D), F32).at[:B].set(c).at[B].set(c_ctx)
    mods = _ada(cond, ada_w, ada_b, tn=_pick(6 * D, 1536))
    mods3 = mods.reshape(R, 1, 6 * D)

    w_in_bf = w_in.astype(BF16)
    cos_t, sin_t = _rope_tables(L, hd)
    tm = _pick(L, 1024)
    x2d = x.reshape(T, D)
    proj = _in_proj(x2d, mods3, lambda i: (i * tm) // L, norm1_g, w_in_bf, cos_t, sin_t,
                    ("q_rope", "k_rope", "plain", "plain", "plain", "plain"), tm, L, hd)
    tmc = _pick(B * Lc, 1024)
    w_ctx = jnp.concatenate([w_in_bf[:, cw:2 * cw], w_in_bf[:, 2 * cw:3 * cw], w_in_bf[:, 4 * cw:5 * cw]], axis=1)
    proj_c = _in_proj(ctx.reshape(B * Lc, D), mods3, lambda i: B, norm1_g, w_ctx, cos_t, sin_t,
                      ("k_plain", "plain", "plain"), tmc, tmc, hd)

    log_g = jnp.log1p(-(2.0 ** (-5.0 - jnp.arange(RET_HEADS, dtype=F32))))
    ret = _retention(log_g, proj, (0, 1, 2, 3), proj_c, (0, 1), ret_norm_g, B, L, Lc, hd)

    w_cat = (0.5 * jnp.concatenate([lru_w_a, lru_w_x], axis=-1)).astype(BF16)
    tL = _pick(L, 64)
    tLc = _pick(Lc, 64)
    h_zero = jnp.zeros((B, Dl), F32)
    lru_args = lambda d: (conv_w, conv_b, w_cat[d], lru_b_a[d], lru_b_x[d], lru_lambda[d])
    proj3 = proj.reshape(B, L, proj.shape[1])
    proj_c3 = proj_c.reshape(B, Lc, proj_c.shape[1])
    _, h_f = _lru(proj_c3, 2, h_zero, *lru_args(0), reverse=False, tL=tLc)
    _, h_b = _lru(proj_c3, 2, h_zero, *lru_args(1), reverse=True, tL=tLc)
    hf3, _ = _lru(proj3, 4, h_f, *lru_args(0), reverse=False, tL=tL)
    lru, _ = _lru(proj3, 4, h_b, *lru_args(1), reverse=True, tL=tL, merge_args=(hf3, 5, lru_norm_g))
    lru = lru.reshape(T, Dl)

    x1, h2, logits = _out_proj(ret, lru, x2d, mods3, norm2_g, w_out.astype(BF16),
                               router_w.astype(BF16), router_b, _pick(L, 256), L)

    gates, eidx, rank, counts = _route(logits, _pick(T, 1024))
    tmE = _pick(T, 1024)
    counts = counts.reshape(E)
    nblk = (counts + tmE - 1) // tmE
    blk_end = jnp.cumsum(nblk)
    pad_start = (blk_end - nblk) * tmE
    NB = (T * TOP_K) // tmE + E
    n_act = blk_end[-1:].astype(jnp.int32)
    blk_e = jnp.minimum(jnp.sum(blk_end[None, :] <= jnp.arange(NB)[:, None], axis=1), E - 1).astype(jnp.int32)
    zrow = (jnp.maximum(blk_end - 1, 0) * tmE).astype(jnp.int32)
    expert_hot = eidx[..., None] == jnp.arange(E, dtype=jnp.int32)
    dest = (jnp.sum(jnp.where(expert_hot, pad_start.astype(jnp.int32), 0), axis=-1) + rank).astype(jnp.int32)

    blk_in_e = jnp.arange(NB) - (blk_end - nblk)[blk_e]
    n_valid = jnp.where(jnp.arange(NB) < n_act[0],
                        jnp.clip(counts[blk_e] - blk_in_e * tmE, 0, tmE), 0).astype(jnp.int32)

    P = NB * tmE
    xs = _dispatch(h2, dest, zrow, nblk.astype(jnp.int32), n_act, P, tmE, _pick(T, 256))
    ys = _moe(blk_e, n_act, n_valid, xs, w_gate, b_gate, w_up, b_up, w_down, b_down, tmE,
              _pick(w_gate.shape[-1], 512))
    out = _combine(dest, x1, gates, mods3, final_norm_g, ys, _pick(T, 256), L)
    return out.reshape(B, L, D)


def kernel(x, c, ctx, c_ctx, ada_w, ada_b, norm1_g, norm2_g, w_in, conv_w, conv_b, lru_w_a, lru_b_a,
           lru_w_x, lru_b_x, lru_lambda, ret_norm_g, lru_norm_g, w_out, router_w, router_b,
           w_gate, b_gate, w_up, b_up, w_down, b_down, final_norm_g):
    assert ada_w.shape[0] == 1, "single-layer configuration"
    return _layer(x, c, ctx, c_ctx, ada_w[0], ada_b[0], norm1_g[0], norm2_g[0], w_in[0], conv_w[0], conv_b[0],
                  lru_w_a[0], lru_b_a[0], lru_w_x[0], lru_b_x[0], lru_lambda[0], ret_norm_g[0], lru_norm_g[0],
                  w_out[0], router_w[0], router_b[0], w_gate[0], b_gate[0], w_up[0], b_up[0], w_down[0],
                  b_down[0], final_norm_g)
```
